```python
import math
import jax
import jax.numpy as jnp
from jax import lax
import numpy as np

D_MODEL = 1024
BATCH = 8
SEQ = 4096
DEPTH = 2

GRID_W = 64
CTX_LEN = 256
N_MOD = 9
D_FF = 2816
EPS = 1e-6
DT_MIN = 1e-3
DT_MAX = 1e-1

GLA_HEADS = 4
GLA_DK = 48
GLA_DV = 96
GLA_RANK = 16
GLA_TAU = 16.0
GLA_CHUNK = 64
GLA_QK = GLA_HEADS * GLA_DK
GLA_WIDTH = GLA_HEADS * GLA_DV

S5_WIDTH = 256
S5_GROUP = 16
S5_GROUPS = S5_WIDTH // S5_GROUP
S5_STATE = 64

SSD_HEADS = 6
SSD_HEADDIM = 64
SSD_GROUPS = 2
SSD_REP = SSD_HEADS // SSD_GROUPS
SSD_STATE = 128
SSD_CONV = 5
SSD_CHUNK = 64
SSD_WIDTH = SSD_HEADS * SSD_HEADDIM
SSD_XBC = SSD_WIDTH + 2 * SSD_GROUPS * SSD_STATE

MIX_WIDTH = GLA_WIDTH + S5_WIDTH + SSD_WIDTH
IN_SPLITS = (GLA_QK, GLA_QK, GLA_WIDTH, GLA_WIDTH, 2 * GLA_RANK, S5_WIDTH, SSD_WIDTH, SSD_XBC, 2 * SSD_HEADS)
IN_WIDTH = sum(IN_SPLITS)

kernel_name = "hybrid_gla_s5_ssd_macaron_prefix"


def rms_norm(x, g):
    xf = x.astype(jnp.float32)
    y = xf * lax.rsqrt(jnp.mean(xf * xf, axis=-1, keepdims=True) + EPS)
    return (y * g.astype(jnp.float32)).astype(x.dtype)


def modulate(x, shift, scale):
    return x * (1.0 + scale) + shift


def swiglu(x, w_gate, w_up, w_down):
    return (jax.nn.silu(x @ w_gate) * (x @ w_up)) @ w_down


def ffn_sublayer(h, mod, g, w_gate, w_up, w_down, base):
    u = modulate(rms_norm(h, g), mod[:, base], mod[:, base + 1])
    return h + 0.5 * mod[:, base + 2] * swiglu(u, w_gate, w_up, w_down)


def flip(t):
    return jnp.flip(t, axis=1)


def to_col_major(t):
    b, n, ch = t.shape
    rows = n // GRID_W
    return t.reshape(b, rows, GRID_W, ch).transpose(0, 2, 1, 3).reshape(b, n, ch)


def from_col_major(t):
    b, n, ch = t.shape
    rows = n // GRID_W
    return t.reshape(b, GRID_W, rows, ch).transpose(0, 2, 1, 3).reshape(b, n, ch)


def split_points():
    pts, acc = [], 0
    for s in IN_SPLITS[:-1]:
        acc += s
        pts.append(acc)
    return pts


def chunk_state_scan(decay, contrib, s0):
    def step(s, inp):
        d, u = inp
        return d * s + u, s
    s_final, starts = lax.scan(step, s0, (decay, contrib))
    return starts, s_final


def run_direction(scan_fn, seqs, s0, reverse):
    if reverse:
        y, s = scan_fn(*[flip(t) for t in seqs], s0)
        return flip(y), s
    return scan_fn(*seqs, s0)


def gla_chunked(q, k, v, log_a, s0):
    b, n, h, dk = q.shape
    dv = v.shape[-1]
    nc = n // GLA_CHUNK
    q = q.reshape(b, nc, GLA_CHUNK, h, dk)
    k = k.reshape(b, nc, GLA_CHUNK, h, dk)
    log_a = log_a.reshape(b, nc, GLA_CHUNK, h, dk)
    v = v.reshape(b, nc, GLA_CHUNK, h, dv)
    cum = jnp.cumsum(log_a, axis=2)
    q_dec = q * jnp.exp(cum)
    k_inv = k * jnp.exp(-cum)
    causal = jnp.tril(jnp.ones((GLA_CHUNK, GLA_CHUNK), dtype=bool))
    scores = jnp.einsum("bcihk,bcjhk->bchij", q_dec, k_inv)
    scores = jnp.where(causal, scores, 0.0)
    o_intra = jnp.einsum("bchij,bcjhv->bcihv", scores, v)
    cum_last = cum[:, :, -1]
    k_end = k * jnp.exp(cum_last[:, :, None] - cum)
    contrib = jnp.einsum("bcjhk,bcjhv->cbhkv", k_end, v)
    decay = jnp.exp(cum_last).transpose(1, 0, 2, 3)[..., None]
    starts, s_final = chunk_state_scan(decay, contrib, s0)
    o_inter = jnp.einsum("bcihk,cbhkv->bcihv", q_dec, starts)
    return (o_intra + o_inter).reshape(b, n, h, dv), s_final


def gla_prep(q, k, v, g_lr, w_gate, b_gate):
    b, n, _ = q.shape
    f32 = jnp.float32
    q = q.astype(f32).reshape(b, n, GLA_HEADS, GLA_DK) * GLA_DK ** -0.5
    k = k.astype(f32).reshape(b, n, GLA_HEADS, GLA_DK)
    v = v.astype(f32).reshape(b, n, GLA_HEADS, GLA_DV)
    g = jnp.einsum("bndr,drk->bndk", g_lr.astype(f32).reshape(b, n, 2, GLA_RANK), w_gate.astype(f32)) + b_gate.astype(f32)
    log_a = (jax.nn.log_sigmoid(g) / GLA_TAU).reshape(b, n, 2, GLA_HEADS, GLA_DK)
    return q, k, v, log_a


def gla_readout(o, r, norm_g):
    b, n = o.shape[:2]
    o = o * lax.rsqrt(jnp.mean(o * o, axis=-1, keepdims=True) + EPS)
    o = o.reshape(b, n, GLA_WIDTH) * norm_g.astype(jnp.float32)
    return (o * jax.nn.silu(r.astype(jnp.float32))).astype(r.dtype)


def gla_mixer(parts_c, parts_l, w_gate, b_gate, norm_g, ctx_out):
    qc, kc, vc, lac = gla_prep(parts_c[0], parts_c[1], parts_c[2], parts_c[4], w_gate, b_gate)
    ql, kl, vl, lal = gla_prep(parts_l[0], parts_l[1], parts_l[2], parts_l[4], w_gate, b_gate)
    zeros = jnp.zeros((qc.shape[0], GLA_HEADS, GLA_DK, GLA_DV), jnp.float32)
    o_c = 0.0
    o_l = 0.0
    for direction, reverse in ((0, False), (1, True)):
        oc, s_ctx = run_direction(gla_chunked, (qc, kc, vc, lac[:, :, direction]), zeros, reverse)
        ol, _ = run_direction(gla_chunked, (ql, kl, vl, lal[:, :, direction]), s_ctx, reverse)
        o_c = o_c + oc
        o_l = o_l + ol
    y_l = gla_readout(o_l, parts_l[3], norm_g)
    y_c = gla_readout(o_c, parts_c[3], norm_g) if ctx_out else None
    return y_c, y_l


def s5_discretize(a_re, a_im, log_dt, b_re, b_im):
    dt = jnp.exp(log_dt)[:, None]
    mag = jnp.exp(dt * a_re)
    ab_re = mag * jnp.cos(dt * a_im)
    ab_im = mag * jnp.sin(dt * a_im)
    den = a_re * a_re + a_im * a_im
    num_re = ab_re - 1.0
    num_im = ab_im
    f_re = (num_re * a_re + num_im * a_im) / den
    f_im = (num_im * a_re - num_re * a_im) / den
    bb_re = f_re[..., None] * b_re - f_im[..., None] * b_im
    bb_im = f_re[..., None] * b_im + f_im[..., None] * b_re
    return ab_re, ab_im, bb_re, bb_im


def complex_affine_combine(e1, e2):
    a1r, a1i, b1r, b1i = e1
    a2r, a2i, b2r, b2i = e2
    return (a2r * a1r - a2i * a1i,
            a2r * a1i + a2i * a1r,
            a2r * b1r - a2i * b1i + b2r,
            a2r * b1i + a2i * b1r + b2i)


def s5_scan(u, ab_re, ab_im, bb_re, bb_im, x0_re, x0_im, reverse):
    bu_re = jnp.einsum("bngh,gph->bngp", u, bb_re)
    bu_im = jnp.einsum("bngh,gph->bngp", u, bb_im)
    first = -1 if reverse else 0
    bu_re = bu_re.at[:, first].add(ab_re * x0_re - ab_im * x0_im)
    bu_im = bu_im.at[:, first].add(ab_re * x0_im + ab_im * x0_re)
    a_re = jnp.broadcast_to(ab_re, bu_re.shape)
    a_im = jnp.broadcast_to(ab_im, bu_im.shape)
    _, _, xr, xi = lax.associative_scan(complex_affine_combine, (a_re, a_im, bu_re, bu_im), reverse=reverse, axis=1)
    return xr, xi


def s5_readout(xr, xi, c_re, c_im):
    return jnp.einsum("bngp,ghp->bngh", xr, c_re) - jnp.einsum("bngp,ghp->bngh", xi, c_im)


def s5_output(y, u, d, w_glu, b_glu, dtype):
    b, n = y.shape[:2]
    f32 = jnp.float32
    y = (y + u * d.astype(f32).reshape(S5_GROUPS, S5_GROUP)).reshape(b, n, S5_WIDTH)
    g = jax.nn.gelu(y)
    return (g * jax.nn.sigmoid(g @ w_glu.astype(f32) + b_glu.astype(f32))).astype(dtype)


def s5_mixer(u_c, u_l, a_re, a_im, log_dt, b_re, b_im, c_re, c_im, d, w_glu, b_glu, ctx_out):
    f32 = jnp.float32
    dtype = u_l.dtype
    bc, nc_, _ = u_c.shape
    bl, nl_, _ = u_l.shape
    uc = u_c.astype(f32).reshape(bc, nc_, S5_GROUPS, S5_GROUP)
    ul = u_l.astype(f32).reshape(bl, nl_, S5_GROUPS, S5_GROUP)
    c_re = c_re.astype(f32)
    c_im = c_im.astype(f32)
    zeros = jnp.zeros((bc, S5_GROUPS, S5_STATE), f32)
    y_c = 0.0
    y_l = 0.0
    for direction, reverse in ((0, False), (1, True)):
        disc = s5_discretize(a_re[direction].astype(f32), a_im[direction].astype(f32),
                             log_dt[direction].astype(f32), b_re.astype(f32), b_im.astype(f32))
        xr_c, xi_c = s5_scan(uc, *disc, zeros, zeros, reverse)
        last = 0 if reverse else -1
        xr_l, xi_l = s5_scan(ul, *disc, xr_c[:, last], xi_c[:, last], reverse)
        y_l = y_l + s5_readout(xr_l, xi_l, c_re, c_im)
        if ctx_out:
            y_c = y_c + s5_readout(xr_c, xi_c, c_re, c_im)
    out_l = s5_output(y_l, ul, d, w_glu, b_glu, dtype)
    out_c = s5_output(y_c, uc, d, w_glu, b_glu, dtype) if ctx_out else None
    return out_c, out_l


def depthwise_conv(x, w):
    ch = x.shape[-1]
    pad = (w.shape[0] - 1) // 2
    return lax.conv_general_dilated(x, w.astype(x.dtype)[:, None, :], window_strides=(1,),
                                    padding=[(pad, pad)], dimension_numbers=("NWC", "WIO", "NWC"),
                                    feature_group_count=ch)


def ssd_chunked(x, log_a, bm, cm, s0):
    b, n, g, r, p = x.shape
    ns = bm.shape[-1]
    nc = n // SSD_CHUNK
    x = x.reshape(b, nc, SSD_CHUNK, g, r, p)
    log_a = log_a.reshape(b, nc, SSD_CHUNK, g, r)
    bm = bm.reshape(b, nc, SSD_CHUNK, g, ns)
    cm = cm.reshape(b, nc, SSD_CHUNK, g, ns)
    cum = jnp.cumsum(log_a, axis=2)
    causal = jnp.tril(jnp.ones((SSD_CHUNK, SSD_CHUNK), dtype=bool))[:, :, None, None]
    seg = cum[:, :, :, None] - cum[:, :, None, :]
    lmat = jnp.exp(jnp.where(causal, seg, -jnp.inf))
    scores = jnp.einsum("bcign,bcjgn->bcijg", cm, bm)
    y_diag = jnp.einsum("bcijgr,bcjgrp->bcigrp", scores[..., None] * lmat, x)
    cum_last = cum[:, :, -1]
    contrib = jnp.einsum("bcjgn,bcjgr,bcjgrp->cbgrpn", bm, jnp.exp(cum_last[:, :, None] - cum), x)
    decay = jnp.exp(cum_last).transpose(1, 0, 2, 3)[..., None, None]
    starts, s_final = chunk_state_scan(decay, contrib, s0)
    y_off = jnp.einsum("bcign,cbgrpn,bcigr->bcigrp", cm, starts, jnp.exp(cum))
    return (y_diag + y_off).reshape(b, n, g, r, p), s_final


def ssd_prep(xbc, dt_raw, conv_w, conv_b, dt_bias, a_log):
    b, n, _ = xbc.shape
    f32 = jnp.float32
    xbc = jax.nn.silu(depthwise_conv(xbc, conv_w) + conv_b.astype(xbc.dtype)).astype(f32)
    xs, bm, cm = jnp.split(xbc, [SSD_WIDTH, SSD_WIDTH + SSD_GROUPS * SSD_STATE], axis=-1)
    xs = xs.reshape(b, n, SSD_GROUPS, SSD_REP, SSD_HEADDIM)
    bm = bm.reshape(b, n, SSD_GROUPS, SSD_STATE)
    cm = cm.reshape(b, n, SSD_GROUPS, SSD_STATE)
    dt = jax.nn.softplus(dt_raw.astype(f32).reshape(b, n, 2, SSD_HEADS) + dt_bias.astype(f32))
    log_a = dt * -jnp.exp(a_log.astype(f32))
    dt = dt.reshape(b, n, 2, SSD_GROUPS, SSD_REP)
    log_a = log_a.reshape(b, n, 2, SSD_GROUPS, SSD_REP)
    return xs, bm, cm, dt, log_a


def ssd_gate_norm(y, z, norm_g):
    b, n = y.shape[:2]
    y = y.reshape(b, n, SSD_WIDTH) * jax.nn.silu(z.astype(jnp.float32))
    return rms_norm(y, norm_g).astype(z.dtype)


def ssd_mixer(parts_c, parts_l, conv_w, conv_b, dt_bias, a_log, d, norm_g, ctx_out):
    z_c, xbc_c, dt_c = parts_c
    z_l, xbc_l, dt_l = (to_col_major(t) for t in parts_l)
    xc, bc, cc, dtc, lac = ssd_prep(xbc_c, dt_c, conv_w, conv_b, dt_bias, a_log)
    xl, bl, cl, dtl, lal = ssd_prep(xbc_l, dt_l, conv_w, conv_b, dt_bias, a_log)
    zeros = jnp.zeros((xc.shape[0], SSD_GROUPS, SSD_REP, SSD_HEADDIM, SSD_STATE), jnp.float32)
    skip = d.astype(jnp.float32).reshape(SSD_GROUPS, SSD_REP, 1)
    y_c = xc * skip
    y_l = xl * skip
    for direction, reverse in ((0, False), (1, True)):
        yc, s_ctx = run_direction(ssd_chunked, (xc * dtc[:, :, direction, ..., None], lac[:, :, direction], bc, cc), zeros, reverse)
        yl, _ = run_direction(ssd_chunked, (xl * dtl[:, :, direction, ..., None], lal[:, :, direction], bl, cl), s_ctx, reverse)
        y_c = y_c + yc
        y_l = y_l + yl
    out_l = from_col_major(ssd_gate_norm(y_l, z_l, norm_g))
    out_c = ssd_gate_norm(y_c, z_c, norm_g) if ctx_out else None
    return out_c, out_l


def token_mixing(u, uc, w_in, w_out, gla_params, s5_params, ssd_params, ctx_out):
    pts = split_points()
    p_l = jnp.split(u @ w_in, pts, axis=-1)
    p_c = jnp.split(uc @ w_in, pts, axis=-1)
    ga_c, ga_l = gla_mixer(p_c[0:5], p_l[0:5], *gla_params, ctx_out)
    sb_c, sb_l = s5_mixer(p_c[5], p_l[5], *s5_params, ctx_out)
    sc_c, sc_l = ssd_mixer(p_c[6:9], p_l[6:9], *ssd_params, ctx_out)
    y_l = jnp.concatenate([ga_l, sb_l, sc_l], axis=-1) @ w_out
    y_c = jnp.concatenate([ga_c, sb_c, sc_c], axis=-1) @ w_out if ctx_out else None
    return y_c, y_l


def setup_inputs(seed: int = 0) -> dict:
    key = jax.random.key(seed)
    ks = iter(jax.random.split(key, 40))

    def nrm(shape, scale):
        return jax.random.normal(next(ks), shape, jnp.float32) * scale

    def near_one(shape):
        return 1.0 + nrm(shape, 0.02)

    nl = DEPTH
    s5_n = jnp.arange(S5_STATE, dtype=jnp.float32)
    s5_log_dt = jax.random.uniform(next(ks), (nl, 2, S5_GROUPS), jnp.float32,
                                   minval=math.log(DT_MIN), maxval=math.log(DT_MAX))
    ssd_dt = jnp.exp(jax.random.uniform(next(ks), (nl, 2, SSD_HEADS), jnp.float32,
                                        minval=math.log(DT_MIN), maxval=math.log(DT_MAX)))
    ssd_a_log = jnp.log(jax.random.uniform(next(ks), (nl, 2, SSD_HEADS), jnp.float32, minval=1.0, maxval=16.0))
    return {
        "x": nrm((BATCH, SEQ, D_MODEL), 1.0),
        "c": nrm((BATCH, D_MODEL), 1.0),
        "ctx": nrm((BATCH, CTX_LEN, D_MODEL), 1.0),
        "c_ctx": nrm((D_MODEL,), 1.0),
        "ada_w": nrm((nl, D_MODEL, N_MOD * D_MODEL), 0.5 * D_MODEL ** -0.5),
        "ada_b": nrm((nl, N_MOD * D_MODEL), 0.02),
        "norm_g": near_one((nl, 3, D_MODEL)),
        "w_in": nrm((nl, D_MODEL, IN_WIDTH), D_MODEL ** -0.5),
        "w_out": nrm((nl, MIX_WIDTH, D_MODEL), MIX_WIDTH ** -0.5),
        "ff_w_gate": nrm((nl, 2, D_MODEL, D_FF), D_MODEL ** -0.5),
        "ff_w_up": nrm((nl, 2, D_MODEL, D_FF), D_MODEL ** -0.5),
        "ff_w_down": nrm((nl, 2, D_FF, D_MODEL), D_FF ** -0.5),
        "gla_w_gate": nrm((nl, 2, GLA_RANK, GLA_QK), GLA_RANK ** -0.5),
        "gla_b_gate": nrm((nl, 2, GLA_QK), 0.1),
        "gla_norm_g": near_one((nl, GLA_WIDTH)),
        "s5_a_re": -0.5 + nrm((nl, 2, S5_GROUPS, S5_STATE), 0.01),
        "s5_a_im": math.pi * s5_n + nrm((nl, 2, S5_GROUPS, S5_STATE), 0.01),
        "s5_log_dt": s5_log_dt,
        "s5_b_re": nrm((nl, S5_GROUPS, S5_STATE, S5_GROUP), (2 * S5_GROUP) ** -0.5),
        "s5_b_im": nrm((nl, S5_GROUPS, S5_STATE, S5_GROUP), (2 * S5_GROUP) ** -0.5),
        "s5_c_re": nrm((nl, S5_GROUPS, S5_GROUP, S5_STATE), (2 * S5_STATE) ** -0.5),
        "s5_c_im": nrm((nl, S5_GROUPS, S5_GROUP, S5_STATE), (2 * S5_STATE) ** -0.5),
        "s5_d": nrm((nl, S5_WIDTH), 1.0),
        "s5_w_glu": nrm((nl, S5_WIDTH, S5_WIDTH), S5_WIDTH ** -0.5),
        "s5_b_glu": nrm((nl, S5_WIDTH), 0.02),
        "ssd_conv_w": nrm((nl, SSD_CONV, SSD_XBC), SSD_CONV ** -0.5),
        "ssd_conv_b": nrm((nl, SSD_XBC), 0.02),
        "ssd_dt_bias": ssd_dt + jnp.log(-jnp.expm1(-ssd_dt)),
        "ssd_a_log": ssd_a_log,
        "ssd_d": 1.0 + nrm((nl, SSD_HEADS), 0.1),
        "ssd_norm_g": near_one((nl, SSD_WIDTH)),
        "final_norm_g": near_one((D_MODEL,)),
    }


def reference(x, c, ctx, c_ctx, ada_w, ada_b, norm_g, w_in, w_out, ff_w_gate, ff_w_up, ff_w_down,
              gla_w_gate, gla_b_gate, gla_norm_g,
              s5_a_re, s5_a_im, s5_log_dt, s5_b_re, s5_b_im, s5_c_re, s5_c_im, s5_d, s5_w_glu, s5_b_glu,
              ssd_conv_w, ssd_conv_b, ssd_dt_bias, ssd_a_log, ssd_d, ssd_norm_g, final_norm_g):
    h = x
    hc = ctx
    for i in range(DEPTH):
        ctx_out = i < DEPTH - 1
        mod = (jax.nn.silu(c) @ ada_w[i] + ada_b[i]).reshape(-1, N_MOD, 1, D_MODEL)
        mod_c = (jax.nn.silu(c_ctx) @ ada_w[i] + ada_b[i]).reshape(1, N_MOD, 1, D_MODEL)
        ff1 = (norm_g[i, 0], ff_w_gate[i, 0], ff_w_up[i, 0], ff_w_down[i, 0])
        ff2 = (norm_g[i, 2], ff_w_gate[i, 1], ff_w_up[i, 1], ff_w_down[i, 1])
        gla_params = (gla_w_gate[i], gla_b_gate[i], gla_norm_g[i])
        s5_params = (s5_a_re[i], s5_a_im[i], s5_log_dt[i], s5_b_re[i], s5_b_im[i],
                     s5_c_re[i], s5_c_im[i], s5_d[i], s5_w_glu[i], s5_b_glu[i])
        ssd_params = (ssd_conv_w[i], ssd_conv_b[i], ssd_dt_bias[i], ssd_a_log[i], ssd_d[i], ssd_norm_g[i])

        h = ffn_sublayer(h, mod, *ff1, 0)
        hc = ffn_sublayer(hc, mod_c, *ff1, 0)

        u = modulate(rms_norm(h, norm_g[i, 1]), mod[:, 3], mod[:, 4])
        uc = modulate(rms_norm(hc, norm_g[i, 1]), mod_c[:, 3], mod_c[:, 4])
        y_c, y_l = token_mixing(u, uc, w_in[i], w_out[i], gla_params, s5_params, ssd_params, ctx_out)
        h = h + mod[:, 5] * y_l

        h = ffn_sublayer(h, mod, *ff2, 6)
        if ctx_out:
            hc = hc + mod_c[:, 5] * y_c
            hc = ffn_sublayer(hc, mod_c, *ff2, 6)
    return rms_norm(h, final_norm_g)
```

```python
import functools
import math

import jax
import jax.numpy as jnp
from jax import lax
from jax.experimental import pallas as pl
from jax.experimental.pallas import tpu as pltpu

F32 = jnp.float32
BF16 = jnp.bfloat16
HI = lax.Precision.HIGHEST

D_MODEL = 1024
GRID_W = 64
N_MOD = 9
D_FF = 2816
EPS = 1e-6

GLA_HEADS = 4
GLA_DK = 48
GLA_DKP = 64
GLA_DV = 96
GLA_RANK = 16
GLA_TAU = 16.0
GLA_QK = GLA_HEADS * GLA_DK
GLA_QKP = GLA_HEADS * GLA_DKP
GLA_WIDTH = GLA_HEADS * GLA_DV

S5_WIDTH = 256
S5_GROUP = 16
S5_GROUPS = 16
S5_STATE = 64
S5_LANES = S5_GROUPS * S5_STATE

SSD_HEADS = 6
SSD_HEADDIM = 64
SSD_GROUPS = 2
SSD_REP = SSD_HEADS // SSD_GROUPS
SSD_STATE = 128
SSD_CONV = 5
SSD_WIDTH = SSD_HEADS * SSD_HEADDIM
SSD_BC = SSD_GROUPS * SSD_STATE
SSD_XBC = SSD_WIDTH + 2 * SSD_BC

MIX_WIDTH = GLA_WIDTH + S5_WIDTH + SSD_WIDTH
IN_SPLITS = (GLA_QK, GLA_QK, GLA_WIDTH, GLA_WIDTH, 2 * GLA_RANK, S5_WIDTH, SSD_WIDTH, SSD_XBC, 2 * SSD_HEADS)

CHUNK = 64
LANE = 128
SUBLANE = 8
HALO = SUBLANE

P_QK = 2 * GLA_QKP
P_GLR = LANE
P_DT = LANE
P_TOTAL = P_QK + 2 * GLA_WIDTH + P_GLR + S5_WIDTH + SSD_WIDTH + SSD_XBC + P_DT

TM_TOKENS = 512
TB_SCAN = 256
TT_S5 = 64
VMEM_LIMIT = 56 * 1024 * 1024


def _silu(x):
    return x * jax.nn.sigmoid(x)


def _rms(x, g):
    return x * lax.rsqrt(jnp.mean(x * x, axis=-1, keepdims=True) + EPS) * g


def _dot(a, b):
    return jnp.dot(a.astype(BF16), b.astype(BF16), preferred_element_type=F32)


def _dot_nt(a, b):
    return lax.dot_general(a.astype(BF16), b.astype(BF16), (((1,), (1,)), ((), ())), preferred_element_type=F32)


def _dot_tn(a, b):
    return lax.dot_general(a.astype(BF16), b.astype(BF16), (((0,), (0,)), ((), ())), preferred_element_type=F32)


def _dot_hi(a, b):
    return jnp.dot(a, b, precision=HI, preferred_element_type=F32)


def _iota(shape, dim):
    return lax.broadcasted_iota(jnp.int32, shape, dim)


def _block_id(shape, dim, width):
    idx = _iota(shape, dim)
    out = jnp.zeros(shape, F32)
    for k in range(1, -(-shape[dim] // width)):
        out = out + jnp.where(idx >= k * width, 1.0, 0.0)
    return out


def _tri(lower):
    i = _iota((CHUNK, CHUNK), 0)
    j = _iota((CHUNK, CHUNK), 1)
    return ((j <= i) if lower else (j >= i)).astype(F32)


def _ada_kernel(c_ref, w_ref, b_ref, o_ref):
    o_ref[0] = _dot_hi(_silu(c_ref[...]), w_ref[0]) + b_ref[0]


def _ada(cc, ada_w, ada_b):
    depth = ada_w.shape[0]
    rows = cc.shape[0]
    tn = D_MODEL
    return pl.pallas_call(
        _ada_kernel,
        grid=(depth, N_MOD * D_MODEL // tn),
        in_specs=[pl.BlockSpec((rows, D_MODEL), lambda l, n: (0, 0)),
                  pl.BlockSpec((1, D_MODEL, tn), lambda l, n: (l, 0, n)),
                  pl.BlockSpec((1, 1, tn), lambda l, n: (l, 0, n))],
        out_specs=pl.BlockSpec((1, rows, tn), lambda l, n: (l, 0, n)),
        out_shape=jax.ShapeDtypeStruct((depth, rows, N_MOD * D_MODEL), F32),
        name="ada_mod",
    )(cc, ada_w, ada_b.reshape(depth, 1, N_MOD * D_MODEL))


def _ffn_body(x, m, g, wg_ref, wu_ref, wd_ref, base):
    u = (_rms(x, g) * (1.0 + m[base + 1:base + 2]) + m[base:base + 1]).astype(BF16)
    gate = jnp.dot(u, wg_ref[...], preferred_element_type=F32)
    up = jnp.dot(u, wu_ref[...], preferred_element_type=F32)
    act = (_silu(gate) * up).astype(BF16)
    y = jnp.dot(act, wd_ref[...], preferred_element_type=F32)
    return x + 0.5 * m[base + 2:base + 3] * y


def _ffn1_kernel(h_ref, mod_ref, ng_ref, wg_ref, wu_ref, wd_ref, o_ref):
    o_ref[...] = _ffn_body(h_ref[...], mod_ref[0], ng_ref[0:1], wg_ref, wu_ref, wd_ref, 0)


def _mix_ffn2_kernel(h_ref, yg_ref, ys_ref, yc_ref, mod_ref, ng_ref, wog_ref, wos_ref, woc_ref,
                     wg_ref, wu_ref, wd_ref, fg_ref, o_ref, *, final):
    m = mod_ref[0]
    mix = (jnp.dot(yg_ref[...].astype(BF16), wog_ref[...], preferred_element_type=F32)
           + jnp.dot(ys_ref[...].astype(BF16), wos_ref[...], preferred_element_type=F32)
           + jnp.dot(yc_ref[...].astype(BF16), woc_ref[...], preferred_element_type=F32))
    x = h_ref[...] + m[5:6] * mix
    x = _ffn_body(x, m, ng_ref[2:3], wg_ref, wu_ref, wd_ref, 6)
    if final:
        x = _rms(x, fg_ref[...])
    o_ref[...] = x


def _const_spec(shape):
    nd = len(shape)
    return pl.BlockSpec(shape, lambda t: (0,) * nd, pipeline_mode=pl.Buffered(1))


def _tile_spec(width):
    return pl.BlockSpec((TM_TOKENS, width), lambda t: (t, 0))


def _mod_spec(tiles_per_row, fixed_row):
    if fixed_row is None:
        return pl.BlockSpec((1, N_MOD, D_MODEL), lambda t: (t // tiles_per_row, 0, 0))
    return pl.BlockSpec((1, N_MOD, D_MODEL), lambda t: (fixed_row, 0, 0))


def _dense_params():
    return pltpu.CompilerParams(dimension_semantics=("arbitrary",), vmem_limit_bytes=VMEM_LIMIT)


def _ffn1(h, mod, mod_spec, ng, wg, wu, wd):
    n = h.shape[0]
    return pl.pallas_call(
        _ffn1_kernel,
        grid=(n // TM_TOKENS,),
        in_specs=[_tile_spec(D_MODEL), mod_spec, _const_spec(ng.shape),
                  _const_spec(wg.shape), _const_spec(wu.shape), _const_spec(wd.shape)],
        out_specs=_tile_spec(D_MODEL),
        out_shape=jax.ShapeDtypeStruct(h.shape, F32),
        compiler_params=_dense_params(),
        name="ffn1",
    )(h, mod, ng, wg, wu, wd)


def _mix_ffn2(h, yg, ys, yc, mod, mod_spec, ng, wog, wos, woc, wg, wu, wd, fg, final):
    n = h.shape[0]
    return pl.pallas_call(
        functools.partial(_mix_ffn2_kernel, final=final),
        grid=(n // TM_TOKENS,),
        in_specs=[_tile_spec(D_MODEL), _tile_spec(GLA_WIDTH), _tile_spec(S5_WIDTH), _tile_spec(SSD_WIDTH),
                  mod_spec, _const_spec(ng.shape), _const_spec(wog.shape), _const_spec(wos.shape),
                  _const_spec(woc.shape), _const_spec(wg.shape), _const_spec(wu.shape), _const_spec(wd.shape),
                  _const_spec(fg.shape)],
        out_specs=_tile_spec(D_MODEL),
        out_shape=jax.ShapeDtypeStruct(h.shape, F32),
        compiler_params=_dense_params(),
        name="mix_ffn2",
    )(h, yg, ys, yc, mod, ng, wog, wos, woc, wg, wu, wd, fg)


_P_WIDTHS = (P_QK, GLA_WIDTH, GLA_WIDTH, P_GLR, S5_WIDTH, SSD_WIDTH, SSD_XBC, P_DT)


def _inproj_kernel(h_ref, mod_ref, ng_ref, w_ref, *out_refs):
    m = mod_ref[0]
    u = _rms(h_ref[...], ng_ref[1:2]) * (1.0 + m[4:5]) + m[3:4]
    p = jnp.dot(u.astype(BF16), w_ref[...], preferred_element_type=F32)
    off = 0
    for ref, width in zip(out_refs, _P_WIDTHS):
        ref[...] = p[:, off:off + width]
        off += width


def _inproj(h, mod, mod_spec, ng, w_all):
    n = h.shape[0]
    return pl.pallas_call(
        _inproj_kernel,
        grid=(n // TM_TOKENS,),
        in_specs=[_tile_spec(D_MODEL), mod_spec, _const_spec(ng.shape), _const_spec(w_all.shape)],
        out_specs=[_tile_spec(w) for w in _P_WIDTHS],
        out_shape=[jax.ShapeDtypeStruct((n, w), F32) for w in _P_WIDTHS],
        compiler_params=_dense_params(),
        name="inproj",
    )(h, mod, ng, w_all)


def _scan_blk(ph, j, nblk):
    return jnp.where(ph == 0, nblk - 1 - j, j)


def _seq_spec(width, nblk):
    return pl.BlockSpec((TB_SCAN, width), lambda b, ph, j: (b * nblk + _scan_blk(ph, j, nblk), 0))


def _seq_out_spec(width, nblk):
    return pl.BlockSpec((TB_SCAN, width), lambda b, ph, j: (b * nblk + jnp.where(ph == 0, 0, j), 0))


def _state_spec(shape):
    return pl.BlockSpec((1,) + shape, lambda b, ph, j: (b,) + (0,) * len(shape))


def _param_spec(shape):
    nd = len(shape)
    return pl.BlockSpec(shape, lambda b, ph, j: (0,) * nd)


def _scan_params():
    return pltpu.CompilerParams(dimension_semantics=("arbitrary", "arbitrary", "arbitrary"),
                                vmem_limit_bytes=VMEM_LIMIT)


def _gla_kernel(qk_ref, v_ref, r_ref, glr_ref, s0_ref, wg_ref, bg_ref, ng_ref,
                y_ref, sfin_ref, st_ref, ob_ref, *, nblk):
    ph = pl.program_id(1)
    j = pl.program_id(2)
    nch = TB_SCAN // CHUNK
    scale = GLA_DK ** -0.5

    lane_k = _block_id((1, GLA_QKP), 1, GLA_DKP)
    lane_v = _block_id((1, GLA_WIDTH), 1, GLA_DV)
    kmask = [(lane_k == h).astype(F32) for h in range(GLA_HEADS)]
    vmask = [(lane_v == h).astype(F32) for h in range(GLA_HEADS)]
    blockmask = (_block_id((GLA_WIDTH, GLA_QKP), 0, GLA_DV)
                 == _block_id((GLA_WIDTH, GLA_QKP), 1, GLA_DKP)).astype(F32)
    tril = _tri(True)
    triu = _tri(False)

    def log_decay_cum(glr, d, tri):
        g = _dot(glr, wg_ref[d]) + bg_ref[d]
        return _dot_hi(tri, jax.nn.log_sigmoid(g) * (1.0 / GLA_TAU))

    def load(c):
        rows = pl.ds(c * CHUNK, CHUNK)
        qk = qk_ref[rows, :]
        return qk[:, :GLA_QKP] * scale, qk[:, GLA_QKP:], v_ref[rows, :], glr_ref[rows, :]

    def advance_state(v, k_end, cum_last):
        st_ref[...] = st_ref[...] * jnp.exp(cum_last) + _dot_tn(v, k_end) * blockmask

    @pl.when(j == 0)
    def _():
        st_ref[...] = jnp.where(ph == 0, s0_ref[0, 1], s0_ref[0, 0])

    @pl.when(ph == 0)
    def _():
        base = (nblk - 1 - j) * TB_SCAN
        for c in reversed(range(nch)):
            q, k, v, glr = load(c)
            cum = log_decay_cum(glr, 1, triu)
            cum_last = cum[0:1, :]
            ob_ref[pl.ds(pl.multiple_of(base + c * CHUNK, CHUNK), CHUNK), :] = _dot_nt(q * jnp.exp(cum), st_ref[...])
            advance_state(v, k * jnp.exp(cum_last - cum), cum_last)

    @pl.when(ph == 1)
    def _():
        base = j * TB_SCAN
        row_i = _iota((GLA_HEADS * CHUNK, CHUNK), 0) & (CHUNK - 1)
        col_j = _iota((GLA_HEADS * CHUNK, CHUNK), 1)
        headsq = (_block_id((GLA_WIDTH, GLA_WIDTH), 0, GLA_DV)
                  == _block_id((GLA_WIDTH, GLA_WIDTH), 1, GLA_DV)).astype(F32) * (1.0 / GLA_DV)
        for c in range(nch):
            q, k, v, glr = load(c)
            cum_f = log_decay_cum(glr, 0, tril)
            cum_b = log_decay_cum(glr, 1, triu)
            cum_last = cum_f[CHUNK - 1:CHUNK, :]
            q_f = q * jnp.exp(cum_f)
            q_b = q * jnp.exp(cum_b)
            stack_f = jnp.concatenate([q_f * kmask[h] for h in range(GLA_HEADS)], axis=0)
            stack_b = jnp.concatenate([q_b * kmask[h] for h in range(GLA_HEADS)], axis=0)
            a_f = _dot_nt(stack_f, k * jnp.exp(-cum_f))
            a_b = _dot_nt(stack_b, k * jnp.exp(-cum_b))
            scores = jnp.where(col_j <= row_i, a_f, 0.0) + jnp.where(col_j >= row_i, a_b, 0.0)
            per_head = _dot(scores, v)
            o = _dot_nt(q_f, st_ref[...])
            o = o + ob_ref[pl.ds(pl.multiple_of(base + c * CHUNK, CHUNK), CHUNK), :]
            for h in range(GLA_HEADS):
                o = o + per_head[h * CHUNK:(h + 1) * CHUNK, :] * vmask[h]
            advance_state(v, k * jnp.exp(cum_last - cum_f), cum_last)
            o = o * lax.rsqrt(_dot_hi(o * o, headsq) + EPS) * ng_ref[...]
            y_ref[pl.ds(c * CHUNK, CHUNK), :] = o * _silu(r_ref[pl.ds(c * CHUNK, CHUNK), :])

    @pl.when(j == nblk - 1)
    def _():
        @pl.when(ph == 0)
        def _():
            sfin_ref[0, 1] = st_ref[...]

        @pl.when(ph == 1)
        def _():
            sfin_ref[0, 0] = st_ref[...]


def _gla(qk, v, r, glr, s0, wg, bg, ng, batch):
    n = qk.shape[0]
    seq = n // batch
    nblk = seq // TB_SCAN
    sshape = (2, GLA_WIDTH, GLA_QKP)
    return pl.pallas_call(
        functools.partial(_gla_kernel, nblk=nblk),
        grid=(batch, 2, nblk),
        in_specs=[_seq_spec(P_QK, nblk), _seq_spec(GLA_WIDTH, nblk), _seq_spec(GLA_WIDTH, nblk),
                  _seq_spec(P_GLR, nblk), _state_spec(sshape),
                  _param_spec(wg.shape), _param_spec(bg.shape), _param_spec(ng.shape)],
        out_specs=[_seq_out_spec(GLA_WIDTH, nblk), _state_spec(sshape)],
        out_shape=[jax.ShapeDtypeStruct((n, GLA_WIDTH), F32), jax.ShapeDtypeStruct((batch,) + sshape, F32)],
        scratch_shapes=[pltpu.VMEM((GLA_WIDTH, GLA_QKP), F32), pltpu.VMEM((seq, GLA_WIDTH), F32)],
        compiler_params=_scan_params(),
        name="gla",
    )(qk, v, r, glr, s0, wg, bg, ng)


def _ssd_kernel(z_ref, x_ref, xp_ref, xn_ref, dt_ref, s0_ref, cw_ref, cb_ref, dtb_ref, alog_ref, dsk_ref, ng_ref,
                y_ref, sfin_ref, st_ref, yb_ref, *, nblk):
    ph = pl.program_id(1)
    j = pl.program_id(2)
    blk = _scan_blk(ph, j, nblk)
    nch = TB_SCAN // CHUNK

    tril = _tri(True)
    triu = _tri(False)
    ci = _iota((CHUNK, CHUNK), 0)
    cj = _iota((CHUNK, CHUNK), 1)
    lane_h = _block_id((1, SSD_WIDTH), 1, SSD_HEADDIM)
    pmask = [(lane_h == h).astype(F32) for h in range(SSD_HEADS)]
    expand = [(_iota((LANE, SSD_WIDTH), 0) == d * SSD_HEADS + _block_id((LANE, SSD_WIDTH), 1, SSD_HEADDIM)).astype(F32)
              for d in range(2)]
    groupmask = (_block_id((SSD_BC, SSD_WIDTH), 0, SSD_STATE)
                 == _block_id((SSD_BC, SSD_WIDTH), 1, SSD_HEADDIM * SSD_REP)).astype(F32)

    prev = jnp.where(blk > 0, xp_ref[...], 0.0)
    nxt = jnp.where(blk < nblk - 1, xn_ref[...], 0.0)
    ext = jnp.concatenate([prev, x_ref[...], nxt], axis=0)
    pad = (SSD_CONV - 1) // 2
    acc = cb_ref[...]
    for t in range(SSD_CONV):
        lo = HALO - pad + t
        acc = acc + ext[lo:lo + TB_SCAN, :] * cw_ref[t:t + 1, :]
    xbc = _silu(acc)

    dt_c = jax.nn.softplus(dt_ref[...] + dtb_ref[...])
    la_c = dt_c * -jnp.exp(alog_ref[...])

    def chunk(c):
        rows = slice(c * CHUNK, (c + 1) * CHUNK)
        xs = xbc[rows, :SSD_WIDTH]
        bm = xbc[rows, SSD_WIDTH:SSD_WIDTH + SSD_BC]
        cm = xbc[rows, SSD_WIDTH + SSD_BC:]
        return xs, bm, cm, dt_c[rows], la_c[rows]

    def advance_state(bm, xw, cum_last):
        st_ref[...] = st_ref[...] * jnp.exp(cum_last) + _dot_tn(bm, xw) * groupmask

    @pl.when(j == 0)
    def _():
        st_ref[...] = jnp.where(ph == 0, s0_ref[0, 1], s0_ref[0, 0])

    @pl.when(ph == 0)
    def _():
        base = blk * TB_SCAN
        for c in reversed(range(nch)):
            xs, bm, cm, dtc, lac = chunk(c)
            cum = _dot_hi(triu, _dot_hi(lac, expand[1]))
            cum_last = cum[0:1, :]
            dte = _dot_hi(dtc, expand[1])
            yb_ref[pl.ds(pl.multiple_of(base + c * CHUNK, CHUNK), CHUNK), :] = _dot(cm, st_ref[...]) * jnp.exp(cum)
            advance_state(bm, xs * jnp.exp(cum_last - cum) * dte, cum_last)

    @pl.when(ph == 1)
    def _():
        base = blk * TB_SCAN
        for c in range(nch):
            xs, bm, cm, dtc, lac = chunk(c)
            cum = _dot_hi(tril, _dot_hi(lac, expand[0]))
            cum_last = cum[CHUNK - 1:CHUNK, :]
            dte = _dot_hi(dtc, expand[0])
            cum_cf = _dot_hi(tril, lac)
            cum_cb = _dot_hi(triu, lac)
            cum_tf = cum_cf.T
            cum_tb = cum_cb.T
            dt_t = dtc.T
            gmat = [_dot_nt(cm[:, g * SSD_STATE:(g + 1) * SSD_STATE], bm[:, g * SSD_STATE:(g + 1) * SSD_STATE])
                    for g in range(SSD_GROUPS)]
            mats = []
            for h in range(SSD_HEADS):
                hb = SSD_HEADS + h
                seg_f = cum_cf[:, h:h + 1] - cum_tf[h:h + 1, :]
                seg_b = cum_cb[:, hb:hb + 1] - cum_tb[hb:hb + 1, :]
                l_f = jnp.exp(jnp.where(cj <= ci, seg_f, -jnp.inf))
                l_b = jnp.exp(jnp.where(cj >= ci, seg_b, -jnp.inf))
                mats.append(gmat[h // SSD_REP] * (l_f * dt_t[h:h + 1, :] + l_b * dt_t[hb:hb + 1, :]))
            per_head = _dot(jnp.concatenate(mats, axis=0), xs)
            y = _dot(cm, st_ref[...]) * jnp.exp(cum)
            y = y + yb_ref[pl.ds(pl.multiple_of(base + c * CHUNK, CHUNK), CHUNK), :] + xs * dsk_ref[...]
            for h in range(SSD_HEADS):
                y = y + per_head[h * CHUNK:(h + 1) * CHUNK, :] * pmask[h]
            advance_state(bm, xs * jnp.exp(cum_last - cum) * dte, cum_last)
            y = y * _silu(z_ref[pl.ds(c * CHUNK, CHUNK), :])
            y_ref[pl.ds(c * CHUNK, CHUNK), :] = _rms(y, ng_ref[...])

    @pl.when(j == nblk - 1)
    def _():
        @pl.when(ph == 0)
        def _():
            sfin_ref[0, 1] = st_ref[...]

        @pl.when(ph == 1)
        def _():
            sfin_ref[0, 0] = st_ref[...]


def _ssd(z, xbc, dt, s0, cw, cb, dtb, alog, dsk, ng, batch):
    n = z.shape[0]
    seq = n // batch
    nblk = seq // TB_SCAN
    tiles = TB_SCAN // HALO
    sshape = (2, SSD_BC, SSD_WIDTH)

    def prev_map(b, ph, j):
        return (jnp.maximum((b * nblk + _scan_blk(ph, j, nblk)) * tiles - 1, 0), 0)

    def next_map(b, ph, j):
        return (jnp.minimum((b * nblk + _scan_blk(ph, j, nblk) + 1) * tiles, n // HALO - 1), 0)

    return pl.pallas_call(
        functools.partial(_ssd_kernel, nblk=nblk),
        grid=(batch, 2, nblk),
        in_specs=[_seq_spec(SSD_WIDTH, nblk), _seq_spec(SSD_XBC, nblk),
                  pl.BlockSpec((HALO, SSD_XBC), prev_map), pl.BlockSpec((HALO, SSD_XBC), next_map),
                  _seq_spec(P_DT, nblk), _state_spec(sshape),
                  _param_spec(cw.shape), _param_spec(cb.shape), _param_spec(dtb.shape), _param_spec(alog.shape),
                  _param_spec(dsk.shape), _param_spec(ng.shape)],
        out_specs=[_seq_out_spec(SSD_WIDTH, nblk), _state_spec(sshape)],
        out_shape=[jax.ShapeDtypeStruct((n, SSD_WIDTH), F32), jax.ShapeDtypeStruct((batch,) + sshape, F32)],
        scratch_shapes=[pltpu.VMEM((SSD_BC, SSD_WIDTH), F32), pltpu.VMEM((seq, SSD_WIDTH), F32)],
        compiler_params=_scan_params(),
        name="ssd",
    )(z, xbc, xbc, xbc, dt, s0, cw, cb, dtb, alog, dsk, ng)


def _s5_kernel(*refs, reverse, final, batch):
    if final:
        (u_ref, x0_ref, bblk_ref, a_ref, cblk_ref, yprev_ref, d_ref, wglu_ref, bglu_ref,
         y_ref, xfin_ref, xs_ref, st_ref) = refs
    else:
        u_ref, x0_ref, bblk_ref, a_ref, cblk_ref, y_ref, xfin_ref, xs_ref, st_ref = refs
    i = pl.program_id(0)
    rows = TT_S5 * batch

    @pl.when(i == 0)
    def _():
        st_ref[...] = x0_ref[...]

    u = u_ref[...].reshape(rows, S5_WIDTH)
    xs_ref[...] = _dot(u, bblk_ref[...])
    a_re = jnp.broadcast_to(a_ref[0:1, :], (batch, S5_LANES))
    a_im = jnp.broadcast_to(a_ref[1:2, :], (batch, S5_LANES))

    def step(s, carry):
        x_re, x_im = carry
        t = (TT_S5 - 1 - s) if reverse else s
        row = pl.ds(pl.multiple_of(t * batch, batch), batch)
        n_re = a_re * x_re - a_im * x_im + xs_ref[row, :S5_LANES]
        n_im = a_re * x_im + a_im * x_re + xs_ref[row, S5_LANES:]
        xs_ref[row, :S5_LANES] = n_re
        xs_ref[row, S5_LANES:] = n_im
        return n_re, n_im

    x_re, x_im = lax.fori_loop(0, TT_S5, step, (st_ref[:, :S5_LANES], st_ref[:, S5_LANES:]))
    st_ref[:, :S5_LANES] = x_re
    st_ref[:, S5_LANES:] = x_im
    xfin_ref[:, :S5_LANES] = x_re
    xfin_ref[:, S5_LANES:] = x_im

    y = _dot(xs_ref[...], cblk_ref[...])
    if final:
        y = y + yprev_ref[...].reshape(rows, S5_WIDTH) + u * d_ref[...]
        g = jax.nn.gelu(y)
        y = g * jax.nn.sigmoid(_dot(g, wglu_ref[...]) + bglu_ref[...])
    y_ref[...] = y.reshape(TT_S5, batch, S5_WIDTH)


def _s5_dir(u, x0, bblk, a, cblk, extra, reverse):
    seq, batch, _ = u.shape
    nblk = seq // TT_S5
    final = extra is not None

    def seq_map(i):
        return ((nblk - 1 - i) if reverse else i, 0, 0)

    def const(shape):
        nd = len(shape)
        return pl.BlockSpec(shape, lambda i: (0,) * nd)

    seq_spec = pl.BlockSpec((TT_S5, batch, S5_WIDTH), seq_map)
    in_specs = [seq_spec, const(x0.shape), const(bblk.shape), const(a.shape), const(cblk.shape)]
    args = [u, x0, bblk, a, cblk]
    if final:
        yprev, d, wglu, bglu = extra
        in_specs += [seq_spec, const(d.shape), const(wglu.shape), const(bglu.shape)]
        args += [yprev, d, wglu, bglu]
    return pl.pallas_call(
        functools.partial(_s5_kernel, reverse=reverse, final=final, batch=batch),
        grid=(nblk,),
        in_specs=in_specs,
        out_specs=[seq_spec, const(x0.shape)],
        out_shape=[jax.ShapeDtypeStruct(u.shape, F32), jax.ShapeDtypeStruct(x0.shape, F32)],
        scratch_shapes=[pltpu.VMEM((TT_S5 * batch, 2 * S5_LANES), F32), pltpu.VMEM(x0.shape, F32)],
        compiler_params=pltpu.CompilerParams(dimension_semantics=("arbitrary",), vmem_limit_bytes=VMEM_LIMIT),
        name="s5_bwd" if reverse else "s5_fwd",
    )(*args)


def _s5(u_tm, x0, prm):
    y_b, xf_b = _s5_dir(u_tm, x0[1], prm["bblk"][1], prm["a"][1], prm["cblk"], None, True)
    y, xf_f = _s5_dir(u_tm, x0[0], prm["bblk"][0], prm["a"][0], prm["cblk"],
                      (y_b, prm["d"], prm["wglu"], prm["bglu"]), False)
    return y, jnp.stack([xf_f, xf_b])


def _s5_params(a_re, a_im, log_dt, b_re, b_im, c_re, c_im, d, w_glu, b_glu):
    dt = jnp.exp(log_dt)[..., None]
    mag = jnp.exp(dt * a_re)
    ab_re = mag * jnp.cos(dt * a_im)
    ab_im = mag * jnp.sin(dt * a_im)
    den = a_re * a_re + a_im * a_im
    num_re = ab_re - 1.0
    num_im = ab_im
    f_re = (num_re * a_re + num_im * a_im) / den
    f_im = (num_im * a_re - num_re * a_im) / den
    bb_re = f_re[..., None] * b_re - f_im[..., None] * b_im
    bb_im = f_re[..., None] * b_im + f_im[..., None] * b_re
    eye = jnp.eye(S5_GROUPS, dtype=F32)

    def in_block(bb):
        return jnp.einsum("dgph,gk->dghkp", bb, eye).reshape(2, S5_WIDTH, S5_LANES)

    def out_block(c):
        return jnp.einsum("ghp,gk->gpkh", c, eye).reshape(S5_LANES, S5_WIDTH)

    return {
        "a": jnp.stack([ab_re.reshape(2, S5_LANES), ab_im.reshape(2, S5_LANES)], axis=1),
        "bblk": jnp.concatenate([in_block(bb_re), in_block(bb_im)], axis=-1).astype(BF16),
        "cblk": jnp.concatenate([out_block(c_re), -out_block(c_im)], axis=0).astype(BF16),
        "d": d.reshape(1, S5_WIDTH),
        "wglu": w_glu.astype(BF16),
        "bglu": b_glu.reshape(1, S5_WIDTH),
    }


def _pad_heads(w, heads, width, padded):
    lead = w.shape[:-1]
    w = w.reshape(lead + (heads, width))
    w = jnp.pad(w, [(0, 0)] * len(lead) + [(0, 0), (0, padded - width)])
    return w.reshape(lead + (heads * padded,))


def _pad_last(w, width):
    return jnp.pad(w, [(0, 0)] * (w.ndim - 1) + [(0, width - w.shape[-1])])


def _pack_w_in(w_in):
    pts = []
    acc = 0
    for s in IN_SPLITS:
        pts.append((acc, acc + s))
        acc += s
    q, k, v, r, glr, s5, z, xbc, dt = (w_in[:, a:b] for a, b in pts)
    return jnp.concatenate([
        _pad_heads(q, GLA_HEADS, GLA_DK, GLA_DKP), _pad_heads(k, GLA_HEADS, GLA_DK, GLA_DKP), v, r,
        _pad_last(glr, P_GLR), s5, z, xbc, _pad_last(dt, P_DT)], axis=-1).astype(BF16)


def _pack_gla_gate(w_gate, b_gate):
    w = _pad_heads(w_gate, GLA_HEADS, GLA_DK, GLA_DKP)
    wp = jnp.zeros((2, P_GLR, GLA_QKP), F32)
    for d in range(2):
        wp = wp.at[d, d * GLA_RANK:(d + 1) * GLA_RANK].set(w[d])
    return wp.astype(BF16), _pad_heads(b_gate, GLA_HEADS, GLA_DK, GLA_DKP).reshape(2, 1, GLA_QKP)


def _to_col_major(t, batch):
    n, ch = t.shape
    rows = n // batch // GRID_W
    return t.reshape(batch, rows, GRID_W, ch).transpose(0, 2, 1, 3).reshape(n, ch)


def _from_col_major(t, batch):
    n, ch = t.shape
    rows = n // batch // GRID_W
    return t.reshape(batch, GRID_W, rows, ch).transpose(0, 2, 1, 3).reshape(n, ch)


def _to_time_major(t, batch):
    n, ch = t.shape
    return t.reshape(batch, n // batch, ch).transpose(1, 0, 2)


def _from_time_major(t):
    seq, batch, ch = t.shape
    return t.transpose(1, 0, 2).reshape(seq * batch, ch)


def kernel(x, c, ctx, c_ctx, ada_w, ada_b, norm_g, w_in, w_out, ff_w_gate, ff_w_up, ff_w_down, gla_w_gate, gla_b_gate, gla_norm_g, s5_a_re, s5_a_im, s5_log_dt, s5_b_re, s5_b_im, s5_c_re, s5_c_im, s5_d, s5_w_glu, s5_b_glu, ssd_conv_w, ssd_conv_b, ssd_dt_bias, ssd_a_log, ssd_d, ssd_norm_g, final_norm_g):
    batch, seq, _ = x.shape
    ctx_len = ctx.shape[1]
    depth = ada_w.shape[0]
    assert seq == GRID_W * CHUNK and ctx_len % TB_SCAN == 0 and seq % TM_TOKENS == 0
    assert batch == SUBLANE and (batch * ctx_len) % TM_TOKENS == 0

    mod_rows = 2 * SUBLANE
    cc = jnp.zeros((mod_rows, D_MODEL), F32).at[:batch].set(c).at[batch].set(c_ctx)
    mods = _ada(cc, ada_w, ada_b).reshape(depth, mod_rows, N_MOD, D_MODEL)
    lat_mod = _mod_spec(seq // TM_TOKENS, None)
    ctx_mod = _mod_spec(None, batch)

    h = x.reshape(batch * seq, D_MODEL)
    hc = ctx.reshape(batch * ctx_len, D_MODEL)
    fg = final_norm_g.reshape(1, D_MODEL)

    for i in range(depth):
        ctx_out = i < depth - 1
        mod = mods[i]
        ng = norm_g[i]
        wg = ff_w_gate[i].astype(BF16)
        wu = ff_w_up[i].astype(BF16)
        wd = ff_w_down[i].astype(BF16)
        w_all = _pack_w_in(w_in[i])
        wo = w_out[i].astype(BF16)
        wog, wos, woc = wo[:GLA_WIDTH], wo[GLA_WIDTH:GLA_WIDTH + S5_WIDTH], wo[GLA_WIDTH + S5_WIDTH:]
        gla_wg, gla_bg = _pack_gla_gate(gla_w_gate[i], gla_b_gate[i])
        gla_ng = gla_norm_g[i].reshape(1, GLA_WIDTH)
        s5p = _s5_params(s5_a_re[i], s5_a_im[i], s5_log_dt[i], s5_b_re[i], s5_b_im[i], s5_c_re[i], s5_c_im[i],
                         s5_d[i], s5_w_glu[i], s5_b_glu[i])
        cw = jnp.pad(ssd_conv_w[i], ((0, SUBLANE - SSD_CONV), (0, 0)))
        cb = ssd_conv_b[i].reshape(1, SSD_XBC)
        dtb = _pad_last(ssd_dt_bias[i].reshape(1, 2 * SSD_HEADS), P_DT)
        alog = _pad_last(ssd_a_log[i].reshape(1, 2 * SSD_HEADS), P_DT)
        dsk = jnp.repeat(ssd_d[i], SSD_HEADDIM).reshape(1, SSD_WIDTH)
        ssd_ng = ssd_norm_g[i].reshape(1, SSD_WIDTH)

        h = _ffn1(h, mod, lat_mod, ng, wg[0], wu[0], wd[0])
        hc = _ffn1(hc, mod, ctx_mod, ng, wg[0], wu[0], wd[0])

        qk_l, v_l, r_l, glr_l, s5_l, z_l, xbc_l, dt_l = _inproj(h, mod, lat_mod, ng, w_all)
        qk_c, v_c, r_c, glr_c, s5_c, z_c, xbc_c, dt_c = _inproj(hc, mod, ctx_mod, ng, w_all)

        gla_s0 = jnp.zeros((batch, 2, GLA_WIDTH, GLA_QKP), F32)
        yg_c, gla_s = _gla(qk_c, v_c, r_c, glr_c, gla_s0, gla_wg, gla_bg, gla_ng, batch)
        yg_l, _ = _gla(qk_l, v_l, r_l, glr_l, gla_s, gla_wg, gla_bg, gla_ng, batch)

        s5_x0 = jnp.zeros((2, batch, 2 * S5_LANES), F32)
        ys_c, s5_x = _s5(_to_time_major(s5_c, batch), s5_x0, s5p)
        ys_l, _ = _s5(_to_time_major(s5_l, batch), s5_x, s5p)
        ys_l = _from_time_major(ys_l)

        ssd_s0 = jnp.zeros((batch, 2, SSD_BC, SSD_WIDTH), F32)
        yc_c, ssd_s = _ssd(z_c, xbc_c, dt_c, ssd_s0, cw, cb, dtb, alog, dsk, ssd_ng, batch)
        yc_l, _ = _ssd(_to_col_major(z_l, batch), _to_col_major(xbc_l, batch), _to_col_major(dt_l, batch),
                       ssd_s, cw, cb, dtb, alog, dsk, ssd_ng, batch)
        yc_l = _from_col_major(yc_l, batch)

        h = _mix_ffn2(h, yg_l, ys_l, yc_l, mod, lat_mod, ng, wog, wos, woc, wg[1], wu[1], wd[1], fg,
                      final=not ctx_out)
        if ctx_out:
            hc = _mix_ffn2(hc, yg_c, _from_time_major(ys_c), yc_c, mod, ctx_mod, ng, wog, wos, woc,
                           wg[1], wu[1], wd[1], fg, final=False)
    return h.reshape(batch, seq, D_MODEL)
```

```python
import functools
import math

import jax
import jax.numpy as jnp
from jax import lax
from jax.experimental import pallas as pl
from jax.experimental.pallas import tpu as pltpu

F32 = jnp.float32
BF16 = jnp.bfloat16
HI = lax.Precision.HIGHEST

D_MODEL = 1024
GRID_W = 64
N_MOD = 9
D_FF = 2816
EPS = 1e-6

GLA_HEADS = 4
GLA_DK = 48
GLA_DKP = 64
GLA_DV = 96
GLA_RANK = 16
GLA_TAU = 16.0
GLA_QK = GLA_HEADS * GLA_DK
GLA_QKP = GLA_HEADS * GLA_DKP
GLA_WIDTH = GLA_HEADS * GLA_DV

S5_WIDTH = 256
S5_GROUP = 16
S5_GROUPS = 16
S5_STATE = 64
S5_LANES = S5_GROUPS * S5_STATE

SSD_HEADS = 6
SSD_HEADDIM = 64
SSD_GROUPS = 2
SSD_REP = SSD_HEADS // SSD_GROUPS
SSD_STATE = 128
SSD_CONV = 5
SSD_WIDTH = SSD_HEADS * SSD_HEADDIM
SSD_BC = SSD_GROUPS * SSD_STATE
SSD_XBC = SSD_WIDTH + 2 * SSD_BC

MIX_WIDTH = GLA_WIDTH + S5_WIDTH + SSD_WIDTH
IN_SPLITS = (GLA_QK, GLA_QK, GLA_WIDTH, GLA_WIDTH, 2 * GLA_RANK, S5_WIDTH, SSD_WIDTH, SSD_XBC, 2 * SSD_HEADS)

CHUNK = 64
LANE = 128
SUBLANE = 8
HALO = SUBLANE

P_QK = 2 * GLA_QKP
P_GLR = LANE
P_DT = LANE
P_TOTAL = P_QK + 2 * GLA_WIDTH + P_GLR + S5_WIDTH + SSD_WIDTH + SSD_XBC + P_DT

TM_TOKENS = 512
TB_SCAN = 256
TT_S5 = 64
VMEM_LIMIT = 56 * 1024 * 1024


def _silu(x):
    return x * jax.nn.sigmoid(x)


def _rms(x, g):
    return x * lax.rsqrt(jnp.mean(x * x, axis=-1, keepdims=True) + EPS) * g


def _dot(a, b):
    return jnp.dot(a.astype(BF16), b.astype(BF16), preferred_element_type=F32)


def _dot_nt(a, b):
    return lax.dot_general(a.astype(BF16), b.astype(BF16), (((1,), (1,)), ((), ())), preferred_element_type=F32)


def _dot_tn(a, b):
    return lax.dot_general(a.astype(BF16), b.astype(BF16), (((0,), (0,)), ((), ())), preferred_element_type=F32)


def _dot_hi(a, b):
    return jnp.dot(a, b, precision=HI, preferred_element_type=F32)


def _split3(x):
    hi = x.astype(BF16)
    rest = x - hi.astype(F32)
    mid = rest.astype(BF16)
    lo = (rest - mid.astype(F32)).astype(BF16)
    return hi, mid, lo


def _dot_exact_rhs(x, m01):
    return jnp.dot(jnp.concatenate(_split3(x), axis=1), jnp.concatenate([m01] * 3, axis=0),
                   preferred_element_type=F32)


def _dot_exact_lhs(m01, x):
    return jnp.dot(jnp.concatenate([m01] * 3, axis=1), jnp.concatenate(_split3(x), axis=0),
                   preferred_element_type=F32)


def _iota(shape, dim):
    return lax.broadcasted_iota(jnp.int32, shape, dim)


def _block_id(shape, dim, width):
    idx = _iota(shape, dim)
    out = jnp.zeros(shape, F32)
    for k in range(1, -(-shape[dim] // width)):
        out = out + jnp.where(idx >= k * width, 1.0, 0.0)
    return out


def _chunk_tri(n, lower):
    i = _iota((n, n), 0)
    j = _iota((n, n), 1)
    same_chunk = (i & -CHUNK) == (j & -CHUNK)
    return jnp.where(same_chunk & ((j <= i) if lower else (j >= i)), 1.0, 0.0).astype(BF16)


def _ada_kernel(c_ref, w_ref, b_ref, o_ref):
    o_ref[0] = _dot_hi(_silu(c_ref[...]), w_ref[0]) + b_ref[0]


def _ada(cc, ada_w, ada_b):
    depth = ada_w.shape[0]
    rows = cc.shape[0]
    tn = D_MODEL
    return pl.pallas_call(
        _ada_kernel,
        grid=(depth, N_MOD * D_MODEL // tn),
        in_specs=[pl.BlockSpec((rows, D_MODEL), lambda l, n: (0, 0)),
                  pl.BlockSpec((1, D_MODEL, tn), lambda l, n: (l, 0, n)),
                  pl.BlockSpec((1, 1, tn), lambda l, n: (l, 0, n))],
        out_specs=pl.BlockSpec((1, rows, tn), lambda l, n: (l, 0, n)),
        out_shape=jax.ShapeDtypeStruct((depth, rows, N_MOD * D_MODEL), F32),
        name="ada_mod",
    )(cc, ada_w, ada_b.reshape(depth, 1, N_MOD * D_MODEL))


def _ffn_body(x, m, g, wg_ref, wu_ref, wd_ref, base):
    u = (_rms(x, g) * (1.0 + m[base + 1:base + 2]) + m[base:base + 1]).astype(BF16)
    gate = jnp.dot(u, wg_ref[...], preferred_element_type=F32)
    up = jnp.dot(u, wu_ref[...], preferred_element_type=F32)
    act = (_silu(gate) * up).astype(BF16)
    y = jnp.dot(act, wd_ref[...], preferred_element_type=F32)
    return x + 0.5 * m[base + 2:base + 3] * y


def _ffn1_kernel(h_ref, mod_ref, ng_ref, wg_ref, wu_ref, wd_ref, o_ref):
    o_ref[...] = _ffn_body(h_ref[...], mod_ref[0], ng_ref[0:1], wg_ref, wu_ref, wd_ref, 0)


def _mix_ffn2_kernel(h_ref, yg_ref, ys_ref, yc_ref, mod_ref, ng_ref, wog_ref, wos_ref, woc_ref,
                     wg_ref, wu_ref, wd_ref, fg_ref, o_ref, *, final):
    m = mod_ref[0]
    mix = (jnp.dot(yg_ref[...].astype(BF16), wog_ref[...], preferred_element_type=F32)
           + jnp.dot(ys_ref[...].astype(BF16), wos_ref[...], preferred_element_type=F32)
           + jnp.dot(yc_ref[...].astype(BF16), woc_ref[...], preferred_element_type=F32))
    x = h_ref[...] + m[5:6] * mix
    x = _ffn_body(x, m, ng_ref[2:3], wg_ref, wu_ref, wd_ref, 6)
    if final:
        x = _rms(x, fg_ref[...])
    o_ref[...] = x


def _const_spec(shape):
    nd = len(shape)
    return pl.BlockSpec(shape, lambda t: (0,) * nd, pipeline_mode=pl.Buffered(1))


def _tile_spec(width):
    return pl.BlockSpec((TM_TOKENS, width), lambda t: (t, 0))


def _mod_spec(tiles_per_row, fixed_row):
    if fixed_row is None:
        return pl.BlockSpec((1, N_MOD, D_MODEL), lambda t: (t // tiles_per_row, 0, 0))
    return pl.BlockSpec((1, N_MOD, D_MODEL), lambda t: (fixed_row, 0, 0))


def _dense_params():
    return pltpu.CompilerParams(dimension_semantics=("arbitrary",), vmem_limit_bytes=VMEM_LIMIT)


def _ffn1(h, mod, mod_spec, ng, wg, wu, wd):
    n = h.shape[0]
    return pl.pallas_call(
        _ffn1_kernel,
        grid=(n // TM_TOKENS,),
        in_specs=[_tile_spec(D_MODEL), mod_spec, _const_spec(ng.shape),
                  _const_spec(wg.shape), _const_spec(wu.shape), _const_spec(wd.shape)],
        out_specs=_tile_spec(D_MODEL),
        out_shape=jax.ShapeDtypeStruct(h.shape, F32),
        compiler_params=_dense_params(),
        name="ffn1",
    )(h, mod, ng, wg, wu, wd)


def _mix_ffn2(h, yg, ys, yc, mod, mod_spec, ng, wog, wos, woc, wg, wu, wd, fg, final):
    n = h.shape[0]
    return pl.pallas_call(
        functools.partial(_mix_ffn2_kernel, final=final),
        grid=(n // TM_TOKENS,),
        in_specs=[_tile_spec(D_MODEL), _tile_spec(GLA_WIDTH), _tile_spec(S5_WIDTH), _tile_spec(SSD_WIDTH),
                  mod_spec, _const_spec(ng.shape), _const_spec(wog.shape), _const_spec(wos.shape),
                  _const_spec(woc.shape), _const_spec(wg.shape), _const_spec(wu.shape), _const_spec(wd.shape),
                  _const_spec(fg.shape)],
        out_specs=_tile_spec(D_MODEL),
        out_shape=jax.ShapeDtypeStruct(h.shape, F32),
        compiler_params=_dense_params(),
        name="mix_ffn2",
    )(h, yg, ys, yc, mod, ng, wog, wos, woc, wg, wu, wd, fg)


_P_WIDTHS = (P_QK, GLA_WIDTH, GLA_WIDTH, P_GLR, S5_WIDTH, SSD_WIDTH, SSD_XBC, P_DT)


def _inproj_kernel(h_ref, mod_ref, ng_ref, w_ref, *out_refs):
    m = mod_ref[0]
    u = _rms(h_ref[...], ng_ref[1:2]) * (1.0 + m[4:5]) + m[3:4]
    p = jnp.dot(u.astype(BF16), w_ref[...], preferred_element_type=F32)
    off = 0
    for ref, width in zip(out_refs, _P_WIDTHS):
        ref[...] = p[:, off:off + width]
        off += width


def _inproj(h, mod, mod_spec, ng, w_all):
    n = h.shape[0]
    return pl.pallas_call(
        _inproj_kernel,
        grid=(n // TM_TOKENS,),
        in_specs=[_tile_spec(D_MODEL), mod_spec, _const_spec(ng.shape), _const_spec(w_all.shape)],
        out_specs=[_tile_spec(w) for w in _P_WIDTHS],
        out_shape=[jax.ShapeDtypeStruct((n, w), F32) for w in _P_WIDTHS],
        compiler_params=_dense_params(),
        name="inproj",
    )(h, mod, ng, w_all)


def _scan_blk(ph, j, nblk):
    return jnp.where(ph == 0, nblk - 1 - j, j)


def _seq_spec(width, nblk):
    return pl.BlockSpec((TB_SCAN, width), lambda b, ph, j: (b * nblk + _scan_blk(ph, j, nblk), 0))


def _seq_out_spec(width, nblk):
    return pl.BlockSpec((TB_SCAN, width), lambda b, ph, j: (b * nblk + jnp.where(ph == 0, 0, j), 0))


def _state_spec(shape):
    return pl.BlockSpec((1,) + shape, lambda b, ph, j: (b,) + (0,) * len(shape))


def _param_spec(shape):
    nd = len(shape)
    return pl.BlockSpec(shape, lambda b, ph, j: (0,) * nd)


def _scan_params():
    return pltpu.CompilerParams(dimension_semantics=("arbitrary", "arbitrary", "arbitrary"),
                                vmem_limit_bytes=VMEM_LIMIT)


def _gla_kernel(qk_ref, v_ref, r_ref, glr_ref, s0_ref, wg_ref, bg_ref, ng_ref,
                y_ref, sfin_ref, st_ref, ob_ref, *, nblk):
    ph = pl.program_id(1)
    j = pl.program_id(2)
    nch = TB_SCAN // CHUNK
    scale = GLA_DK ** -0.5

    lane_k = _block_id((1, GLA_QKP), 1, GLA_DKP)
    lane_v = _block_id((1, GLA_WIDTH), 1, GLA_DV)
    kmask = [(lane_k == h).astype(F32) for h in range(GLA_HEADS)]
    vmask = [(lane_v == h).astype(F32) for h in range(GLA_HEADS)]
    blockmask = (_block_id((GLA_WIDTH, GLA_QKP), 0, GLA_DV)
                 == _block_id((GLA_WIDTH, GLA_QKP), 1, GLA_DKP)).astype(F32)

    qk = qk_ref[...]
    q = qk[:, :GLA_QKP] * scale
    k = qk[:, GLA_QKP:]
    v = v_ref[...]
    glr = glr_ref[...]

    def decay_cum(d):
        g = _dot(glr, wg_ref[d]) + bg_ref[d]
        return _dot_exact_lhs(_chunk_tri(TB_SCAN, lower=(d == 0)), jax.nn.log_sigmoid(g) * (1.0 / GLA_TAU))

    def rows(x, c):
        return x[c * CHUNK:(c + 1) * CHUNK]

    @pl.when(j == 0)
    def _():
        st_ref[...] = jnp.where(ph == 0, s0_ref[0, 1], s0_ref[0, 0])

    @pl.when(ph == 0)
    def _():
        base = (nblk - 1 - j) * TB_SCAN
        cum = decay_cum(1)
        q_dec = q * jnp.exp(cum)
        k_inv = k * jnp.exp(-cum)
        state = st_ref[...]
        for c in reversed(range(nch)):
            ob_ref[pl.ds(pl.multiple_of(base + c * CHUNK, CHUNK), CHUNK), :] = _dot_nt(rows(q_dec, c), state)
            decay = jnp.exp(rows(cum, c)[0:1, :])
            state = state * decay + _dot_tn(rows(v, c), rows(k_inv, c) * decay) * blockmask
        st_ref[...] = state

    @pl.when(ph == 1)
    def _():
        base = pl.multiple_of(j * TB_SCAN, TB_SCAN)
        row_i = _iota((GLA_HEADS * CHUNK, CHUNK), 0) & (CHUNK - 1)
        col_j = _iota((GLA_HEADS * CHUNK, CHUNK), 1)
        same_head = (_block_id((GLA_WIDTH, GLA_WIDTH), 0, GLA_DV)
                     == _block_id((GLA_WIDTH, GLA_WIDTH), 1, GLA_DV)).astype(BF16)
        cum_f = decay_cum(0)
        cum_b = decay_cum(1)
        q_f = q * jnp.exp(cum_f)
        q_b = q * jnp.exp(cum_b)
        k_f = k * jnp.exp(-cum_f)
        k_b = k * jnp.exp(-cum_b)
        state = st_ref[...]
        outs = []
        for c in range(nch):
            stack_f = jnp.concatenate([rows(q_f, c) * kmask[h] for h in range(GLA_HEADS)], axis=0)
            stack_b = jnp.concatenate([rows(q_b, c) * kmask[h] for h in range(GLA_HEADS)], axis=0)
            a_f = _dot_nt(stack_f, rows(k_f, c))
            a_b = _dot_nt(stack_b, rows(k_b, c))
            scores = jnp.where(col_j <= row_i, a_f, 0.0) + jnp.where(col_j >= row_i, a_b, 0.0)
            per_head = _dot(scores, rows(v, c))
            o = _dot_nt(rows(q_f, c), state)
            for h in range(GLA_HEADS):
                o = o + per_head[h * CHUNK:(h + 1) * CHUNK, :] * vmask[h]
            outs.append(o)
            decay = jnp.exp(rows(cum_f, c)[CHUNK - 1:CHUNK, :])
            state = state * decay + _dot_tn(rows(v, c), rows(k_f, c) * decay) * blockmask
        st_ref[...] = state
        o = jnp.concatenate(outs, axis=0) + ob_ref[pl.ds(base, TB_SCAN), :]
        mean_sq = _dot_exact_rhs(o * o, same_head) * (1.0 / GLA_DV)
        y_ref[...] = o * lax.rsqrt(mean_sq + EPS) * ng_ref[...] * _silu(r_ref[...])

    @pl.when(j == nblk - 1)
    def _():
        @pl.when(ph == 0)
        def _():
            sfin_ref[0, 1] = st_ref[...]

        @pl.when(ph == 1)
        def _():
            sfin_ref[0, 0] = st_ref[...]


def _gla(qk, v, r, glr, s0, wg, bg, ng, batch):
    n = qk.shape[0]
    seq = n // batch
    nblk = seq // TB_SCAN
    sshape = (2, GLA_WIDTH, GLA_QKP)
    return pl.pallas_call(
        functools.partial(_gla_kernel, nblk=nblk),
        grid=(batch, 2, nblk),
        in_specs=[_seq_spec(P_QK, nblk), _seq_spec(GLA_WIDTH, nblk), _seq_spec(GLA_WIDTH, nblk),
                  _seq_spec(P_GLR, nblk), _state_spec(sshape),
                  _param_spec(wg.shape), _param_spec(bg.shape), _param_spec(ng.shape)],
        out_specs=[_seq_out_spec(GLA_WIDTH, nblk), _state_spec(sshape)],
        out_shape=[jax.ShapeDtypeStruct((n, GLA_WIDTH), F32), jax.ShapeDtypeStruct((batch,) + sshape, F32)],
        scratch_shapes=[pltpu.VMEM((GLA_WIDTH, GLA_QKP), F32), pltpu.VMEM((seq, GLA_WIDTH), F32)],
        compiler_params=_scan_params(),
        name="gla",
    )(qk, v, r, glr, s0, wg, bg, ng)


def _ssd_kernel(z_ref, x_ref, xp_ref, xn_ref, dt_ref, s0_ref, cw_ref, cb_ref, dtb_ref, alog_ref, dsk_ref, ng_ref,
                y_ref, sfin_ref, st_ref, yb_ref, ext_ref, *, nblk):
    ph = pl.program_id(1)
    j = pl.program_id(2)
    blk = _scan_blk(ph, j, nblk)
    base = pl.multiple_of(blk * TB_SCAN, TB_SCAN)
    nch = TB_SCAN // CHUNK

    ci = _iota((CHUNK, CHUNK), 0)
    cj = _iota((CHUNK, CHUNK), 1)
    lane_h = _block_id((1, SSD_WIDTH), 1, SSD_HEADDIM)
    pmask = [(lane_h == h).astype(F32) for h in range(SSD_HEADS)]
    groupmask = (_block_id((SSD_BC, SSD_WIDTH), 0, SSD_STATE)
                 == _block_id((SSD_BC, SSD_WIDTH), 1, SSD_HEADDIM * SSD_REP)).astype(F32)

    def expand(d):
        return jnp.where(_iota((LANE, SSD_WIDTH), 0) == d * SSD_HEADS + _block_id((LANE, SSD_WIDTH), 1, SSD_HEADDIM),
                         1.0, 0.0).astype(BF16)

    def rows(x, c):
        return x[c * CHUNK:(c + 1) * CHUNK]

    ext_ref[0:HALO, :] = jnp.where(blk > 0, xp_ref[...], 0.0)
    ext_ref[HALO:HALO + TB_SCAN, :] = x_ref[...]
    ext_ref[HALO + TB_SCAN:, :] = jnp.where(blk < nblk - 1, xn_ref[...], 0.0)
    pad = (SSD_CONV - 1) // 2
    acc = cb_ref[...]
    for t in range(SSD_CONV):
        acc = acc + ext_ref[pl.ds(HALO - pad + t, TB_SCAN), :] * cw_ref[t:t + 1, :]
    xbc = _silu(acc)
    xs = xbc[:, :SSD_WIDTH]
    bm = xbc[:, SSD_WIDTH:SSD_WIDTH + SSD_BC]
    cm = xbc[:, SSD_WIDTH + SSD_BC:]

    dt_c = jax.nn.softplus(dt_ref[...] + dtb_ref[...])
    la_c = dt_c * -jnp.exp(alog_ref[...])

    def scan_terms(d, cum_c):
        both = _dot_exact_rhs(jnp.concatenate([cum_c, dt_c], axis=0), expand(d))
        cum = both[:TB_SCAN].reshape(nch, CHUNK, SSD_WIDTH)
        last = cum[:, CHUNK - 1:CHUNK, :] if d == 0 else cum[:, 0:1, :]
        weight = jnp.exp(last - cum).reshape(TB_SCAN, SSD_WIDTH) * both[TB_SCAN:]
        return jnp.exp(cum).reshape(TB_SCAN, SSD_WIDTH), weight, jnp.exp(last)

    @pl.when(j == 0)
    def _():
        st_ref[...] = jnp.where(ph == 0, s0_ref[0, 1], s0_ref[0, 0])

    @pl.when(ph == 0)
    def _():
        cum_c = _dot_exact_lhs(_chunk_tri(TB_SCAN, lower=False), la_c)
        grow, weight, decay = scan_terms(1, cum_c)
        xw = xs * weight
        state = st_ref[...]
        outs = [None] * nch
        for c in reversed(range(nch)):
            outs[c] = _dot(rows(cm, c), state)
            state = state * decay[c] + _dot_tn(rows(bm, c), rows(xw, c)) * groupmask
        st_ref[...] = state
        yb_ref[pl.ds(base, TB_SCAN), :] = jnp.concatenate(outs, axis=0) * grow

    @pl.when(ph == 1)
    def _():
        cum_cf = _dot_exact_lhs(_chunk_tri(TB_SCAN, lower=True), la_c)
        cum_cb = _dot_exact_lhs(_chunk_tri(TB_SCAN, lower=False), la_c)
        grow, weight, decay = scan_terms(0, cum_cf)
        xw = xs * weight
        dt_lane = 2 * SUBLANE
        lane = _iota((TB_SCAN, LANE), 1)
        packed = jnp.where(lane < SSD_HEADS, cum_cf,
                           jnp.where(lane < 2 * SSD_HEADS, cum_cb, pltpu.roll(dt_c, dt_lane, axis=1)))
        state = st_ref[...]
        outs = []
        for c in range(nch):
            packed_t = rows(packed, c).T
            cm_c = rows(cm, c)
            bm_c = rows(bm, c)
            gmat = [_dot_nt(cm_c[:, g * SSD_STATE:(g + 1) * SSD_STATE], bm_c[:, g * SSD_STATE:(g + 1) * SSD_STATE])
                    for g in range(SSD_GROUPS)]
            mats = []
            for h in range(SSD_HEADS):
                hb = SSD_HEADS + h
                seg_f = rows(cum_cf, c)[:, h:h + 1] - packed_t[h:h + 1, :]
                seg_b = rows(cum_cb, c)[:, hb:hb + 1] - packed_t[hb:hb + 1, :]
                l_f = jnp.exp(jnp.where(cj <= ci, seg_f, -jnp.inf))
                l_b = jnp.exp(jnp.where(cj >= ci, seg_b, -jnp.inf))
                mats.append(gmat[h // SSD_REP] * (l_f * packed_t[dt_lane + h:dt_lane + h + 1, :]
                                                  + l_b * packed_t[dt_lane + hb:dt_lane + hb + 1, :]))
            per_head = _dot(jnp.concatenate(mats, axis=0), rows(xs, c))
            y = _dot(cm_c, state) * rows(grow, c)
            for h in range(SSD_HEADS):
                y = y + per_head[h * CHUNK:(h + 1) * CHUNK, :] * pmask[h]
            outs.append(y)
            state = state * decay[c] + _dot_tn(bm_c, rows(xw, c)) * groupmask
        st_ref[...] = state
        y = jnp.concatenate(outs, axis=0) + yb_ref[pl.ds(base, TB_SCAN), :] + xs * dsk_ref[...]
        y_ref[...] = _rms(y * _silu(z_ref[...]), ng_ref[...])

    @pl.when(j == nblk - 1)
    def _():
        @pl.when(ph == 0)
        def _():
            sfin_ref[0, 1] = st_ref[...]

        @pl.when(ph == 1)
        def _():
            sfin_ref[0, 0] = st_ref[...]


def _ssd(z, xbc, dt, s0, cw, cb, dtb, alog, dsk, ng, batch):
    n = z.shape[0]
    seq = n // batch
    nblk = seq // TB_SCAN
    tiles = TB_SCAN // HALO
    sshape = (2, SSD_BC, SSD_WIDTH)

    def prev_map(b, ph, j):
        return (jnp.maximum((b * nblk + _scan_blk(ph, j, nblk)) * tiles - 1, 0), 0)

    def next_map(b, ph, j):
        return (jnp.minimum((b * nblk + _scan_blk(ph, j, nblk) + 1) * tiles, n // HALO - 1), 0)

    return pl.pallas_call(
        functools.partial(_ssd_kernel, nblk=nblk),
        grid=(batch, 2, nblk),
        in_specs=[_seq_spec(SSD_WIDTH, nblk), _seq_spec(SSD_XBC, nblk),
                  pl.BlockSpec((HALO, SSD_XBC), prev_map), pl.BlockSpec((HALO, SSD_XBC), next_map),
                  _seq_spec(P_DT, nblk), _state_spec(sshape),
                  _param_spec(cw.shape), _param_spec(cb.shape), _param_spec(dtb.shape), _param_spec(alog.shape),
                  _param_spec(dsk.shape), _param_spec(ng.shape)],
        out_specs=[_seq_out_spec(SSD_WIDTH, nblk), _state_spec(sshape)],
        out_shape=[jax.ShapeDtypeStruct((n, SSD_WIDTH), F32), jax.ShapeDtypeStruct((batch,) + sshape, F32)],
        scratch_shapes=[pltpu.VMEM((SSD_BC, SSD_WIDTH), F32), pltpu.VMEM((seq, SSD_WIDTH), F32),
                        pltpu.VMEM((TB_SCAN + 2 * HALO, SSD_XBC), F32)],
        compiler_params=_scan_params(),
        name="ssd",
    )(z, xbc, xbc, xbc, dt, s0, cw, cb, dtb, alog, dsk, ng)


def _s5_kernel(*refs, reverse, final, batch):
    if final:
        (u_ref, x0_ref, bblk_ref, a_ref, cblk_ref, yprev_ref, d_ref, wglu_ref, bglu_ref,
         y_ref, xfin_ref, xs_ref, st_ref) = refs
    else:
        u_ref, x0_ref, bblk_ref, a_ref, cblk_ref, y_ref, xfin_ref, xs_ref, st_ref = refs
    i = pl.program_id(0)
    rows = TT_S5 * batch

    @pl.when(i == 0)
    def _():
        st_ref[...] = x0_ref[...]

    u = u_ref[...].reshape(rows, S5_WIDTH)
    xs_ref[...] = _dot(u, bblk_ref[...])
    a_re = jnp.broadcast_to(a_ref[0:1, :], (batch, S5_LANES))
    a_im = jnp.broadcast_to(a_ref[1:2, :], (batch, S5_LANES))

    def step(s, carry):
        x_re, x_im = carry
        t = (TT_S5 - 1 - s) if reverse else s
        row = pl.ds(pl.multiple_of(t * batch, batch), batch)
        n_re = a_re * x_re - a_im * x_im + xs_ref[row, :S5_LANES]
        n_im = a_re * x_im + a_im * x_re + xs_ref[row, S5_LANES:]
        xs_ref[row, :S5_LANES] = n_re
        xs_ref[row, S5_LANES:] = n_im
        return n_re, n_im

    x_re, x_im = lax.fori_loop(0, TT_S5, step, (st_ref[:, :S5_LANES], st_ref[:, S5_LANES:]))
    st_ref[:, :S5_LANES] = x_re
    st_ref[:, S5_LANES:] = x_im
    xfin_ref[:, :S5_LANES] = x_re
    xfin_ref[:, S5_LANES:] = x_im

    y = _dot(xs_ref[...], cblk_ref[...])
    if final:
        y = y + yprev_ref[...].reshape(rows, S5_WIDTH) + u * d_ref[...]
        g = jax.nn.gelu(y)
        y = g * jax.nn.sigmoid(_dot(g, wglu_ref[...]) + bglu_ref[...])
    y_ref[...] = y.reshape(TT_S5, batch, S5_WIDTH)


def _s5_dir(u, x0, bblk, a, cblk, extra, reverse):
    seq, batch, _ = u.shape
    nblk = seq // TT_S5
    final = extra is not None

    def seq_map(i):
        return ((nblk - 1 - i) if reverse else i, 0, 0)

    def const(shape):
        nd = len(shape)
        return pl.BlockSpec(shape, lambda i: (0,) * nd)

    seq_spec = pl.BlockSpec((TT_S5, batch, S5_WIDTH), seq_map)
    in_specs = [seq_spec, const(x0.shape), const(bblk.shape), const(a.shape), const(cblk.shape)]
    args = [u, x0, bblk, a, cblk]
    if final:
        yprev, d, wglu, bglu = extra
        in_specs += [seq_spec, const(d.shape), const(wglu.shape), const(bglu.shape)]
        args += [yprev, d, wglu, bglu]
    return pl.pallas_call(
        functools.partial(_s5_kernel, reverse=reverse, final=final, batch=batch),
        grid=(nblk,),
        in_specs=in_specs,
        out_specs=[seq_spec, const(x0.shape)],
        out_shape=[jax.ShapeDtypeStruct(u.shape, F32), jax.ShapeDtypeStruct(x0.shape, F32)],
        scratch_shapes=[pltpu.VMEM((TT_S5 * batch, 2 * S5_LANES), F32), pltpu.VMEM(x0.shape, F32)],
        compiler_params=pltpu.CompilerParams(dimension_semantics=("arbitrary",), vmem_limit_bytes=VMEM_LIMIT),
        name="s5_bwd" if reverse else "s5_fwd",
    )(*args)


def _s5(u_tm, x0, prm):
    y_b, xf_b = _s5_dir(u_tm, x0[1], prm["bblk"][1], prm["a"][1], prm["cblk"], None, True)
    y, xf_f = _s5_dir(u_tm, x0[0], prm["bblk"][0], prm["a"][0], prm["cblk"],
                      (y_b, prm["d"], prm["wglu"], prm["bglu"]), False)
    return y, jnp.stack([xf_f, xf_b])


def _s5_params(a_re, a_im, log_dt, b_re, b_im, c_re, c_im, d, w_glu, b_glu):
    dt = jnp.exp(log_dt)[..., None]
    mag = jnp.exp(dt * a_re)
    ab_re = mag * jnp.cos(dt * a_im)
    ab_im = mag * jnp.sin(dt * a_im)
    den = a_re * a_re + a_im * a_im
    num_re = ab_re - 1.0
    num_im = ab_im
    f_re = (num_re * a_re + num_im * a_im) / den
    f_im = (num_im * a_re - num_re * a_im) / den
    bb_re = f_re[..., None] * b_re - f_im[..., None] * b_im
    bb_im = f_re[..., None] * b_im + f_im[..., None] * b_re
    eye = jnp.eye(S5_GROUPS, dtype=F32)

    def in_block(bb):
        return jnp.einsum("dgph,gk->dghkp", bb, eye).reshape(2, S5_WIDTH, S5_LANES)

    def out_block(c):
        return jnp.einsum("ghp,gk->gpkh", c, eye).reshape(S5_LANES, S5_WIDTH)

    return {
        "a": jnp.stack([ab_re.reshape(2, S5_LANES), ab_im.reshape(2, S5_LANES)], axis=1),
        "bblk": jnp.concatenate([in_block(bb_re), in_block(bb_im)], axis=-1).astype(BF16),
        "cblk": jnp.concatenate([out_block(c_re), -out_block(c_im)], axis=0).astype(BF16),
        "d": d.reshape(1, S5_WIDTH),
        "wglu": w_glu.astype(BF16),
        "bglu": b_glu.reshape(1, S5_WIDTH),
    }


def _pad_heads(w, heads, width, padded):
    lead = w.shape[:-1]
    w = w.reshape(lead + (heads, width))
    w = jnp.pad(w, [(0, 0)] * len(lead) + [(0, 0), (0, padded - width)])
    return w.reshape(lead + (heads * padded,))


def _pad_last(w, width):
    return jnp.pad(w, [(0, 0)] * (w.ndim - 1) + [(0, width - w.shape[-1])])


def _pack_w_in(w_in):
    pts = []
    acc = 0
    for s in IN_SPLITS:
        pts.append((acc, acc + s))
        acc += s
    q, k, v, r, glr, s5, z, xbc, dt = (w_in[:, a:b] for a, b in pts)
    return jnp.concatenate([
        _pad_heads(q, GLA_HEADS, GLA_DK, GLA_DKP), _pad_heads(k, GLA_HEADS, GLA_DK, GLA_DKP), v, r,
        _pad_last(glr, P_GLR), s5, z, xbc, _pad_last(dt, P_DT)], axis=-1).astype(BF16)


def _pack_gla_gate(w_gate, b_gate):
    w = _pad_heads(w_gate, GLA_HEADS, GLA_DK, GLA_DKP)
    wp = jnp.zeros((2, P_GLR, GLA_QKP), F32)
    for d in range(2):
        wp = wp.at[d, d * GLA_RANK:(d + 1) * GLA_RANK].set(w[d])
    return wp.astype(BF16), _pad_heads(b_gate, GLA_HEADS, GLA_DK, GLA_DKP).reshape(2, 1, GLA_QKP)


def _to_col_major(t, batch):
    n, ch = t.shape
    rows = n // batch // GRID_W
    return t.reshape(batch, rows, GRID_W, ch).transpose(0, 2, 1, 3).reshape(n, ch)


def _from_col_major(t, batch):
    n, ch = t.shape
    rows = n // batch // GRID_W
    return t.reshape(batch, GRID_W, rows, ch).transpose(0, 2, 1, 3).reshape(n, ch)


def _to_time_major(t, batch):
    n, ch = t.shape
    return t.reshape(batch, n // batch, ch).transpose(1, 0, 2)


def _from_time_major(t):
    seq, batch, ch = t.shape
    return t.transpose(1, 0, 2).reshape(seq * batch, ch)


def kernel(x, c, ctx, c_ctx, ada_w, ada_b, norm_g, w_in, w_out, ff_w_gate, ff_w_up, ff_w_down, gla_w_gate, gla_b_gate, gla_norm_g, s5_a_re, s5_a_im, s5_log_dt, s5_b_re, s5_b_im, s5_c_re, s5_c_im, s5_d, s5_w_glu, s5_b_glu, ssd_conv_w, ssd_conv_b, ssd_dt_bias, ssd_a_log, ssd_d, ssd_norm_g, final_norm_g):
    batch, seq, _ = x.shape
    ctx_len = ctx.shape[1]
    depth = ada_w.shape[0]
    assert seq == GRID_W * CHUNK and ctx_len % TB_SCAN == 0 and seq % TM_TOKENS == 0
    assert batch == SUBLANE and (batch * ctx_len) % TM_TOKENS == 0

    mod_rows = 2 * SUBLANE
    cc = jnp.zeros((mod_rows, D_MODEL), F32).at[:batch].set(c).at[batch].set(c_ctx)
    mods = _ada(cc, ada_w, ada_b).reshape(depth, mod_rows, N_MOD, D_MODEL)
    lat_mod = _mod_spec(seq // TM_TOKENS, None)
    ctx_mod = _mod_spec(None, batch)

    h = x.reshape(batch * seq, D_MODEL)
    hc = ctx.reshape(batch * ctx_len, D_MODEL)
    fg = final_norm_g.reshape(1, D_MODEL)

    for i in range(depth):
        ctx_out = i < depth - 1
        mod = mods[i]
        ng = norm_g[i]
        wg = ff_w_gate[i].astype(BF16)
        wu = ff_w_up[i].astype(BF16)
        wd = ff_w_down[i].astype(BF16)
        w_all = _pack_w_in(w_in[i])
        wo = w_out[i].astype(BF16)
        wog, wos, woc = wo[:GLA_WIDTH], wo[GLA_WIDTH:GLA_WIDTH + S5_WIDTH], wo[GLA_WIDTH + S5_WIDTH:]
        gla_wg, gla_bg = _pack_gla_gate(gla_w_gate[i], gla_b_gate[i])
        gla_ng = gla_norm_g[i].reshape(1, GLA_WIDTH)
        s5p = _s5_params(s5_a_re[i], s5_a_im[i], s5_log_dt[i], s5_b_re[i], s5_b_im[i], s5_c_re[i], s5_c_im[i],
                         s5_d[i], s5_w_glu[i], s5_b_glu[i])
        cw = jnp.pad(ssd_conv_w[i], ((0, SUBLANE - SSD_CONV), (0, 0)))
        cb = ssd_conv_b[i].reshape(1, SSD_XBC)
        dtb = _pad_last(ssd_dt_bias[i].reshape(1, 2 * SSD_HEADS), P_DT)
        alog = _pad_last(ssd_a_log[i].reshape(1, 2 * SSD_HEADS), P_DT)
        dsk = jnp.repeat(ssd_d[i], SSD_HEADDIM).reshape(1, SSD_WIDTH)
        ssd_ng = ssd_norm_g[i].reshape(1, SSD_WIDTH)

        h = _ffn1(h, mod, lat_mod, ng, wg[0], wu[0], wd[0])
        hc = _ffn1(hc, mod, ctx_mod, ng, wg[0], wu[0], wd[0])

        qk_l, v_l, r_l, glr_l, s5_l, z_l, xbc_l, dt_l = _inproj(h, mod, lat_mod, ng, w_all)
        qk_c, v_c, r_c, glr_c, s5_c, z_c, xbc_c, dt_c = _inproj(hc, mod, ctx_mod, ng, w_all)

        gla_s0 = jnp.zeros((batch, 2, GLA_WIDTH, GLA_QKP), F32)
        yg_c, gla_s = _gla(qk_c, v_c, r_c, glr_c, gla_s0, gla_wg, gla_bg, gla_ng, batch)
        yg_l, _ = _gla(qk_l, v_l, r_l, glr_l, gla_s, gla_wg, gla_bg, gla_ng, batch)

        s5_x0 = jnp.zeros((2, batch, 2 * S5_LANES), F32)
        ys_c, s5_x = _s5(_to_time_major(s5_c, batch), s5_x0, s5p)
        ys_l, _ = _s5(_to_time_major(s5_l, batch), s5_x, s5p)
        ys_l = _from_time_major(ys_l)

        ssd_s0 = jnp.zeros((batch, 2, SSD_BC, SSD_WIDTH), F32)
        yc_c, ssd_s = _ssd(z_c, xbc_c, dt_c, ssd_s0, cw, cb, dtb, alog, dsk, ssd_ng, batch)
        yc_l, _ = _ssd(_to_col_major(z_l, batch), _to_col_major(xbc_l, batch), _to_col_major(dt_l, batch),
                       ssd_s, cw, cb, dtb, alog, dsk, ssd_ng, batch)
        yc_l = _from_col_major(yc_l, batch)

        h = _mix_ffn2(h, yg_l, ys_l, yc_l, mod, lat_mod, ng, wog, wos, woc, wg[1], wu[1], wd[1], fg,
                      final=not ctx_out)
        if ctx_out:
            hc = _mix_ffn2(hc, yg_c, _from_time_major(ys_c), yc_c, mod, ctx_mod, ng, wog, wos, woc,
                           wg[1], wu[1], wd[1], fg, final=False)
    return h.reshape(batch, seq, D_MODEL)
```

```python
import functools
import math

import jax
import jax.numpy as jnp
from jax import lax
from jax.experimental import pallas as pl
from jax.experimental.pallas import tpu as pltpu

F32 = jnp.float32
BF16 = jnp.bfloat16
HI = lax.Precision.HIGHEST

D_MODEL = 1024
GRID_W = 64
N_MOD = 9
D_FF = 2816
EPS = 1e-6

GLA_HEADS = 4
GLA_DK = 48
GLA_DKP = 64
GLA_DV = 96
GLA_RANK = 16
GLA_TAU = 16.0
GLA_QK = GLA_HEADS * GLA_DK
GLA_QKP = GLA_HEADS * GLA_DKP
GLA_WIDTH = GLA_HEADS * GLA_DV

S5_WIDTH = 256
S5_GROUP = 16
S5_GROUPS = 16
S5_STATE = 64
S5_LANES = S5_GROUPS * S5_STATE

SSD_HEADS = 6
SSD_HEADDIM = 64
SSD_GROUPS = 2
SSD_REP = SSD_HEADS // SSD_GROUPS
SSD_STATE = 128
SSD_CONV = 5
SSD_WIDTH = SSD_HEADS * SSD_HEADDIM
SSD_BC = SSD_GROUPS * SSD_STATE
SSD_XBC = SSD_WIDTH + 2 * SSD_BC

MIX_WIDTH = GLA_WIDTH + S5_WIDTH + SSD_WIDTH
IN_SPLITS = (GLA_QK, GLA_QK, GLA_WIDTH, GLA_WIDTH, 2 * GLA_RANK, S5_WIDTH, SSD_WIDTH, SSD_XBC, 2 * SSD_HEADS)

CHUNK = 64
LANE = 128
SUBLANE = 8
HALO = SUBLANE

P_QK = 2 * GLA_QKP
P_GLR = LANE
P_DT = LANE
P_TOTAL = P_QK + 2 * GLA_WIDTH + P_GLR + S5_WIDTH + SSD_WIDTH + SSD_XBC + P_DT

TM_TOKENS = 512
TB_SCAN = 256
TT_S5 = 128
VMEM_LIMIT = 56 * 1024 * 1024


def _silu(x):
    return x * jax.nn.sigmoid(x)


def _rms(x, g):
    return x * lax.rsqrt(jnp.mean(x * x, axis=-1, keepdims=True) + EPS) * g


def _dot(a, b):
    return jnp.dot(a.astype(BF16), b.astype(BF16), preferred_element_type=F32)


def _dot_nt(a, b):
    return lax.dot_general(a.astype(BF16), b.astype(BF16), (((1,), (1,)), ((), ())), preferred_element_type=F32)


def _dot_tn(a, b):
    return lax.dot_general(a.astype(BF16), b.astype(BF16), (((0,), (0,)), ((), ())), preferred_element_type=F32)


def _dot_hi(a, b):
    return jnp.dot(a, b, precision=HI, preferred_element_type=F32)


def _split3(x):
    hi = x.astype(BF16)
    rest = x - hi.astype(F32)
    mid = rest.astype(BF16)
    lo = (rest - mid.astype(F32)).astype(BF16)
    return hi, mid, lo


def _dot_exact_rhs(x, m01):
    return jnp.dot(jnp.concatenate(_split3(x), axis=1), jnp.concatenate([m01] * 3, axis=0),
                   preferred_element_type=F32)


def _dot_exact_lhs(m01, x):
    return jnp.dot(jnp.concatenate([m01] * 3, axis=1), jnp.concatenate(_split3(x), axis=0),
                   preferred_element_type=F32)


def _iota(shape, dim):
    return lax.broadcasted_iota(jnp.int32, shape, dim)


def _block_id(shape, dim, width):
    idx = _iota(shape, dim)
    out = jnp.zeros(shape, F32)
    for k in range(1, -(-shape[dim] // width)):
        out = out + jnp.where(idx >= k * width, 1.0, 0.0)
    return out


def _chunk_tri(n, lower):
    i = _iota((n, n), 0)
    j = _iota((n, n), 1)
    same_chunk = (i & -CHUNK) == (j & -CHUNK)
    return jnp.where(same_chunk & ((j <= i) if lower else (j >= i)), 1.0, 0.0).astype(BF16)


def _ada_kernel(c_ref, w_ref, b_ref, o_ref):
    o_ref[0] = _dot_hi(_silu(c_ref[...]), w_ref[0]) + b_ref[0]


def _ada(cc, ada_w, ada_b):
    depth = ada_w.shape[0]
    rows = cc.shape[0]
    tn = D_MODEL
    return pl.pallas_call(
        _ada_kernel,
        grid=(depth, N_MOD * D_MODEL // tn),
        in_specs=[pl.BlockSpec((rows, D_MODEL), lambda l, n: (0, 0)),
                  pl.BlockSpec((1, D_MODEL, tn), lambda l, n: (l, 0, n)),
                  pl.BlockSpec((1, 1, tn), lambda l, n: (l, 0, n))],
        out_specs=pl.BlockSpec((1, rows, tn), lambda l, n: (l, 0, n)),
        out_shape=jax.ShapeDtypeStruct((depth, rows, N_MOD * D_MODEL), F32),
        name="ada_mod",
    )(cc, ada_w, ada_b.reshape(depth, 1, N_MOD * D_MODEL))


def _ffn_body(x, m, g, wg_ref, wu_ref, wd_ref, base):
    u = (_rms(x, g) * (1.0 + m[base + 1:base + 2]) + m[base:base + 1]).astype(BF16)
    gate = jnp.dot(u, wg_ref[...], preferred_element_type=F32)
    up = jnp.dot(u, wu_ref[...], preferred_element_type=F32)
    act = (_silu(gate) * up).astype(BF16)
    y = jnp.dot(act, wd_ref[...], preferred_element_type=F32)
    return x + 0.5 * m[base + 2:base + 3] * y


def _ffn1_kernel(h_ref, mod_ref, ng_ref, wg_ref, wu_ref, wd_ref, o_ref):
    o_ref[...] = _ffn_body(h_ref[...], mod_ref[0], ng_ref[0:1], wg_ref, wu_ref, wd_ref, 0)


def _mix_ffn2_kernel(h_ref, yg_ref, ys_ref, yc_ref, mod_ref, ng_ref, wog_ref, wos_ref, woc_ref,
                     wg_ref, wu_ref, wd_ref, fg_ref, o_ref, *, final):
    m = mod_ref[0]
    mix = (jnp.dot(yg_ref[...].astype(BF16), wog_ref[...], preferred_element_type=F32)
           + jnp.dot(ys_ref[...].astype(BF16), wos_ref[...], preferred_element_type=F32)
           + jnp.dot(yc_ref[...].astype(BF16), woc_ref[...], preferred_element_type=F32))
    x = h_ref[...] + m[5:6] * mix
    x = _ffn_body(x, m, ng_ref[2:3], wg_ref, wu_ref, wd_ref, 6)
    if final:
        x = _rms(x, fg_ref[...])
    o_ref[...] = x


def _const_spec(shape):
    nd = len(shape)
    return pl.BlockSpec(shape, lambda t: (0,) * nd, pipeline_mode=pl.Buffered(1))


def _tile_spec(width):
    return pl.BlockSpec((TM_TOKENS, width), lambda t: (t, 0))


def _mod_spec(tiles_per_row, fixed_row):
    if fixed_row is None:
        return pl.BlockSpec((1, N_MOD, D_MODEL), lambda t: (t // tiles_per_row, 0, 0))
    return pl.BlockSpec((1, N_MOD, D_MODEL), lambda t: (fixed_row, 0, 0))


def _dense_params():
    return pltpu.CompilerParams(dimension_semantics=("arbitrary",), vmem_limit_bytes=VMEM_LIMIT)


def _ffn1(h, mod, mod_spec, ng, wg, wu, wd):
    n = h.shape[0]
    return pl.pallas_call(
        _ffn1_kernel,
        grid=(n // TM_TOKENS,),
        in_specs=[_tile_spec(D_MODEL), mod_spec, _const_spec(ng.shape),
                  _const_spec(wg.shape), _const_spec(wu.shape), _const_spec(wd.shape)],
        out_specs=_tile_spec(D_MODEL),
        out_shape=jax.ShapeDtypeStruct(h.shape, F32),
        compiler_params=_dense_params(),
        name="ffn1",
    )(h, mod, ng, wg, wu, wd)


def _mix_ffn2(h, yg, ys, yc, mod, mod_spec, ng, wog, wos, woc, wg, wu, wd, fg, final):
    n = h.shape[0]
    return pl.pallas_call(
        functools.partial(_mix_ffn2_kernel, final=final),
        grid=(n // TM_TOKENS,),
        in_specs=[_tile_spec(D_MODEL), _tile_spec(GLA_WIDTH), _tile_spec(S5_WIDTH), _tile_spec(SSD_WIDTH),
                  mod_spec, _const_spec(ng.shape), _const_spec(wog.shape), _const_spec(wos.shape),
                  _const_spec(woc.shape), _const_spec(wg.shape), _const_spec(wu.shape), _const_spec(wd.shape),
                  _const_spec(fg.shape)],
        out_specs=_tile_spec(D_MODEL),
        out_shape=jax.ShapeDtypeStruct(h.shape, F32),
        compiler_params=_dense_params(),
        name="mix_ffn2",
    )(h, yg, ys, yc, mod, ng, wog, wos, woc, wg, wu, wd, fg)


_P_WIDTHS = (P_QK, GLA_WIDTH, GLA_WIDTH, P_GLR, S5_WIDTH, SSD_WIDTH, SSD_XBC, P_DT)


def _inproj_kernel(h_ref, mod_ref, ng_ref, w_ref, *out_refs):
    m = mod_ref[0]
    u = _rms(h_ref[...], ng_ref[1:2]) * (1.0 + m[4:5]) + m[3:4]
    p = jnp.dot(u.astype(BF16), w_ref[...], preferred_element_type=F32)
    off = 0
    for ref, width in zip(out_refs, _P_WIDTHS):
        ref[...] = p[:, off:off + width]
        off += width


def _inproj(h, mod, mod_spec, ng, w_all):
    n = h.shape[0]
    return pl.pallas_call(
        _inproj_kernel,
        grid=(n // TM_TOKENS,),
        in_specs=[_tile_spec(D_MODEL), mod_spec, _const_spec(ng.shape), _const_spec(w_all.shape)],
        out_specs=[_tile_spec(w) for w in _P_WIDTHS],
        out_shape=[jax.ShapeDtypeStruct((n, w), F32) for w in _P_WIDTHS],
        compiler_params=_dense_params(),
        name="inproj",
    )(h, mod, ng, w_all)


def _scan_blk(ph, j, nblk):
    return jnp.where(ph == 0, nblk - 1 - j, j)


def _seq_spec(width, nblk):
    return pl.BlockSpec((TB_SCAN, width), lambda b, ph, j: (b * nblk + _scan_blk(ph, j, nblk), 0))


def _phase0_blk(ph, j, nblk):
    return jnp.where(ph == 0, nblk - 1 - j, 0)


def _seq_phase0_spec(width, nblk):
    return pl.BlockSpec((TB_SCAN, width), lambda b, ph, j: (b * nblk + _phase0_blk(ph, j, nblk), 0))


def _seq_phase1_spec(width, nblk):
    return pl.BlockSpec((TB_SCAN, width), lambda b, ph, j: (b * nblk + jnp.where(ph == 0, 0, j), 0))


def _state_spec(shape):
    return pl.BlockSpec((1,) + shape, lambda b, ph, j: (b,) + (0,) * len(shape))


def _param_spec(shape):
    nd = len(shape)
    return pl.BlockSpec(shape, lambda b, ph, j: (0,) * nd)


def _scan_params():
    return pltpu.CompilerParams(dimension_semantics=("arbitrary", "arbitrary", "arbitrary"),
                                vmem_limit_bytes=VMEM_LIMIT)


def _gla_kernel(qk_ref, v_ref, r_ref, glr_ref, s0_ref, wg_ref, bg_ref, ng_ref,
                y_ref, sfin_ref, st_ref, ob_ref, qkb_ref, *, nblk):
    ph = pl.program_id(1)
    j = pl.program_id(2)
    nch = TB_SCAN // CHUNK
    scale = GLA_DK ** -0.5

    lane_k = _block_id((1, GLA_QKP), 1, GLA_DKP)
    lane_v = _block_id((1, GLA_WIDTH), 1, GLA_DV)
    kmask = [(lane_k == h).astype(F32) for h in range(GLA_HEADS)]
    vmask = [(lane_v == h).astype(F32) for h in range(GLA_HEADS)]
    blockmask = (_block_id((GLA_WIDTH, GLA_QKP), 0, GLA_DV)
                 == _block_id((GLA_WIDTH, GLA_QKP), 1, GLA_DKP)).astype(F32)

    qk = qk_ref[...]
    q = qk[:, :GLA_QKP] * scale
    k = qk[:, GLA_QKP:]
    v = v_ref[...]
    glr = glr_ref[...]

    def decay_cum(d):
        g = _dot(glr, wg_ref[d]) + bg_ref[d]
        return _dot_exact_lhs(_chunk_tri(TB_SCAN, lower=(d == 0)), jax.nn.log_sigmoid(g) * (1.0 / GLA_TAU))

    def rows(x, c):
        return x[c * CHUNK:(c + 1) * CHUNK]

    @pl.when(j == 0)
    def _():
        st_ref[...] = jnp.where(ph == 0, s0_ref[0, 1], s0_ref[0, 0])

    @pl.when(ph == 0)
    def _():
        base = pl.multiple_of((nblk - 1 - j) * TB_SCAN, TB_SCAN)
        cum = decay_cum(1)
        q_dec = (q * jnp.exp(cum)).astype(BF16)
        k_inv = k * jnp.exp(-cum)
        qkb_ref[pl.ds(base, TB_SCAN), :GLA_QKP] = q_dec
        qkb_ref[pl.ds(base, TB_SCAN), GLA_QKP:] = k_inv.astype(BF16)
        state = st_ref[...]
        for c in reversed(range(nch)):
            ob_ref[pl.ds(base + c * CHUNK, CHUNK), :] = _dot_nt(rows(q_dec, c), state)
            decay = jnp.exp(rows(cum, c)[0:1, :])
            state = state * decay + _dot_tn(rows(v, c), rows(k_inv, c) * decay) * blockmask
        st_ref[...] = state

    @pl.when(ph == 1)
    def _():
        base = pl.multiple_of(j * TB_SCAN, TB_SCAN)
        row_i = _iota((GLA_HEADS * CHUNK, CHUNK), 0) & (CHUNK - 1)
        col_j = _iota((GLA_HEADS * CHUNK, CHUNK), 1)
        same_head = (_block_id((GLA_WIDTH, GLA_WIDTH), 0, GLA_DV)
                     == _block_id((GLA_WIDTH, GLA_WIDTH), 1, GLA_DV)).astype(BF16)
        cum_f = decay_cum(0)
        q_f = q * jnp.exp(cum_f)
        k_f = k * jnp.exp(-cum_f)
        q_b = qkb_ref[pl.ds(base, TB_SCAN), :GLA_QKP]
        k_b = qkb_ref[pl.ds(base, TB_SCAN), GLA_QKP:]
        kmask16 = [m.astype(BF16) for m in kmask]
        state = st_ref[...]
        outs = []
        for c in range(nch):
            stack_f = jnp.concatenate([rows(q_f, c) * kmask[h] for h in range(GLA_HEADS)], axis=0)
            stack_b = jnp.concatenate([rows(q_b, c) * kmask16[h] for h in range(GLA_HEADS)], axis=0)
            a_f = _dot_nt(stack_f, rows(k_f, c))
            a_b = _dot_nt(stack_b, rows(k_b, c))
            scores = jnp.where(col_j <= row_i, a_f, 0.0) + jnp.where(col_j >= row_i, a_b, 0.0)
            per_head = _dot(scores, rows(v, c))
            o = _dot_nt(rows(q_f, c), state)
            for h in range(GLA_HEADS):
                o = o + per_head[h * CHUNK:(h + 1) * CHUNK, :] * vmask[h]
            outs.append(o)
            decay = jnp.exp(rows(cum_f, c)[CHUNK - 1:CHUNK, :])
            state = state * decay + _dot_tn(rows(v, c), rows(k_f, c) * decay) * blockmask
        st_ref[...] = state
        o = jnp.concatenate(outs, axis=0) + ob_ref[pl.ds(base, TB_SCAN), :]
        mean_sq = _dot_exact_rhs(o * o, same_head) * (1.0 / GLA_DV)
        y_ref[...] = o * lax.rsqrt(mean_sq + EPS) * ng_ref[...] * _silu(r_ref[...])

    @pl.when(j == nblk - 1)
    def _():
        @pl.when(ph == 0)
        def _():
            sfin_ref[0, 1] = st_ref[...]

        @pl.when(ph == 1)
        def _():
            sfin_ref[0, 0] = st_ref[...]


def _gla(qk, v, r, glr, s0, wg, bg, ng, batch):
    n = qk.shape[0]
    seq = n // batch
    nblk = seq // TB_SCAN
    sshape = (2, GLA_WIDTH, GLA_QKP)
    return pl.pallas_call(
        functools.partial(_gla_kernel, nblk=nblk),
        grid=(batch, 2, nblk),
        in_specs=[_seq_spec(P_QK, nblk), _seq_spec(GLA_WIDTH, nblk), _seq_phase1_spec(GLA_WIDTH, nblk),
                  _seq_spec(P_GLR, nblk), _state_spec(sshape),
                  _param_spec(wg.shape), _param_spec(bg.shape), _param_spec(ng.shape)],
        out_specs=[_seq_phase1_spec(GLA_WIDTH, nblk), _state_spec(sshape)],
        out_shape=[jax.ShapeDtypeStruct((n, GLA_WIDTH), F32), jax.ShapeDtypeStruct((batch,) + sshape, F32)],
        scratch_shapes=[pltpu.VMEM((GLA_WIDTH, GLA_QKP), F32), pltpu.VMEM((seq, GLA_WIDTH), F32),
                        pltpu.VMEM((seq, P_QK), BF16)],
        compiler_params=_scan_params(),
        name="gla",
    )(qk, v, r, glr, s0, wg, bg, ng)


def _ssd_kernel(z_ref, x_ref, xp_ref, xn_ref, dt_ref, s0_ref, cw_ref, cb_ref, dtb_ref, alog_ref, dsk_ref, ng_ref,
                y_ref, sfin_ref, st_ref, yb_ref, ext_ref, xc_ref, cumb_ref, *, nblk):
    ph = pl.program_id(1)
    j = pl.program_id(2)
    blk = _scan_blk(ph, j, nblk)
    base = pl.multiple_of(blk * TB_SCAN, TB_SCAN)
    nch = TB_SCAN // CHUNK

    ci = _iota((CHUNK, CHUNK), 0)
    cj = _iota((CHUNK, CHUNK), 1)
    lane_h = _block_id((1, SSD_WIDTH), 1, SSD_HEADDIM)
    pmask = [(lane_h == h).astype(F32) for h in range(SSD_HEADS)]
    groupmask = (_block_id((SSD_BC, SSD_WIDTH), 0, SSD_STATE)
                 == _block_id((SSD_BC, SSD_WIDTH), 1, SSD_HEADDIM * SSD_REP)).astype(F32)

    def expand(d):
        return jnp.where(_iota((LANE, SSD_WIDTH), 0) == d * SSD_HEADS + _block_id((LANE, SSD_WIDTH), 1, SSD_HEADDIM),
                         1.0, 0.0).astype(BF16)

    def rows(x, c):
        return x[c * CHUNK:(c + 1) * CHUNK]

    def conv_silu():
        ext_ref[0:HALO, :] = jnp.where(blk > 0, xp_ref[...], 0.0)
        ext_ref[HALO:HALO + TB_SCAN, :] = x_ref[...]
        ext_ref[HALO + TB_SCAN:, :] = jnp.where(blk < nblk - 1, xn_ref[...], 0.0)
        pad = (SSD_CONV - 1) // 2
        acc = cb_ref[...]
        for t in range(SSD_CONV):
            acc = acc + ext_ref[pl.ds(HALO - pad + t, TB_SCAN), :] * cw_ref[t:t + 1, :]
        return _silu(acc)

    def split_xbc(xbc):
        return xbc[:, :SSD_WIDTH], xbc[:, SSD_WIDTH:SSD_WIDTH + SSD_BC], xbc[:, SSD_WIDTH + SSD_BC:]

    dt_c = jax.nn.softplus(dt_ref[...] + dtb_ref[...])
    la_c = dt_c * -jnp.exp(alog_ref[...])

    def scan_terms(d, cum_c):
        both = _dot_exact_rhs(jnp.concatenate([cum_c, dt_c], axis=0), expand(d))
        cum = both[:TB_SCAN].reshape(nch, CHUNK, SSD_WIDTH)
        last = cum[:, CHUNK - 1:CHUNK, :] if d == 0 else cum[:, 0:1, :]
        weight = jnp.exp(last - cum).reshape(TB_SCAN, SSD_WIDTH) * both[TB_SCAN:]
        return jnp.exp(cum).reshape(TB_SCAN, SSD_WIDTH), weight, jnp.exp(last)

    @pl.when(j == 0)
    def _():
        st_ref[...] = jnp.where(ph == 0, s0_ref[0, 1], s0_ref[0, 0])

    @pl.when(ph == 0)
    def _():
        xbc = conv_silu()
        cum_c = _dot_exact_lhs(_chunk_tri(TB_SCAN, lower=False), la_c)
        xc_ref[pl.ds(base, TB_SCAN), :] = xbc
        cumb_ref[pl.ds(base, TB_SCAN), :] = cum_c
        xs, bm, cm = split_xbc(xbc)
        grow, weight, decay = scan_terms(1, cum_c)
        xw = xs * weight
        state = st_ref[...]
        outs = [None] * nch
        for c in reversed(range(nch)):
            outs[c] = _dot(rows(cm, c), state)
            state = state * decay[c] + _dot_tn(rows(bm, c), rows(xw, c)) * groupmask
        st_ref[...] = state
        yb_ref[pl.ds(base, TB_SCAN), :] = jnp.concatenate(outs, axis=0) * grow

    @pl.when(ph == 1)
    def _():
        xs, bm, cm = split_xbc(xc_ref[pl.ds(base, TB_SCAN), :])
        cum_cf = _dot_exact_lhs(_chunk_tri(TB_SCAN, lower=True), la_c)
        cum_cb = cumb_ref[pl.ds(base, TB_SCAN), :]
        grow, weight, decay = scan_terms(0, cum_cf)
        xw = xs * weight
        dt_lane = 2 * SUBLANE
        lane = _iota((TB_SCAN, LANE), 1)
        packed = jnp.where(lane < SSD_HEADS, cum_cf,
                           jnp.where(lane < 2 * SSD_HEADS, cum_cb, pltpu.roll(dt_c, dt_lane, axis=1)))
        state = st_ref[...]
        outs = []
        for c in range(nch):
            packed_t = rows(packed, c).T
            cm_c = rows(cm, c)
            bm_c = rows(bm, c)
            gmat = [_dot_nt(cm_c[:, g * SSD_STATE:(g + 1) * SSD_STATE], bm_c[:, g * SSD_STATE:(g + 1) * SSD_STATE])
                    for g in range(SSD_GROUPS)]
            mats = []
            for h in range(SSD_HEADS):
                hb = SSD_HEADS + h
                seg_f = rows(cum_cf, c)[:, h:h + 1] - packed_t[h:h + 1, :]
                seg_b = rows(cum_cb, c)[:, hb:hb + 1] - packed_t[hb:hb + 1, :]
                l_f = jnp.exp(jnp.where(cj <= ci, seg_f, -jnp.inf))
                l_b = jnp.exp(jnp.where(cj >= ci, seg_b, -jnp.inf))
                mats.append(gmat[h // SSD_REP] * (l_f * packed_t[dt_lane + h:dt_lane + h + 1, :]
                                                  + l_b * packed_t[dt_lane + hb:dt_lane + hb + 1, :]))
            per_head = _dot(jnp.concatenate(mats, axis=0), rows(xs, c))
            y = _dot(cm_c, state) * rows(grow, c)
            for h in range(SSD_HEADS):
                y = y + per_head[h * CHUNK:(h + 1) * CHUNK, :] * pmask[h]
            outs.append(y)
            state = state * decay[c] + _dot_tn(bm_c, rows(xw, c)) * groupmask
        st_ref[...] = state
        y = jnp.concatenate(outs, axis=0) + yb_ref[pl.ds(base, TB_SCAN), :] + xs * dsk_ref[...]
        y_ref[...] = _rms(y * _silu(z_ref[...]), ng_ref[...])

    @pl.when(j == nblk - 1)
    def _():
        @pl.when(ph == 0)
        def _():
            sfin_ref[0, 1] = st_ref[...]

        @pl.when(ph == 1)
        def _():
            sfin_ref[0, 0] = st_ref[...]


def _ssd(z, xbc, dt, s0, cw, cb, dtb, alog, dsk, ng, batch):
    n = z.shape[0]
    seq = n // batch
    nblk = seq // TB_SCAN
    tiles = TB_SCAN // HALO
    sshape = (2, SSD_BC, SSD_WIDTH)

    def prev_map(b, ph, j):
        return (jnp.maximum((b * nblk + _phase0_blk(ph, j, nblk)) * tiles - 1, 0), 0)

    def next_map(b, ph, j):
        return (jnp.minimum((b * nblk + _phase0_blk(ph, j, nblk) + 1) * tiles, n // HALO - 1), 0)

    return pl.pallas_call(
        functools.partial(_ssd_kernel, nblk=nblk),
        grid=(batch, 2, nblk),
        in_specs=[_seq_phase1_spec(SSD_WIDTH, nblk), _seq_phase0_spec(SSD_XBC, nblk),
                  pl.BlockSpec((HALO, SSD_XBC), prev_map), pl.BlockSpec((HALO, SSD_XBC), next_map),
                  _seq_spec(P_DT, nblk), _state_spec(sshape),
                  _param_spec(cw.shape), _param_spec(cb.shape), _param_spec(dtb.shape), _param_spec(alog.shape),
                  _param_spec(dsk.shape), _param_spec(ng.shape)],
        out_specs=[_seq_phase1_spec(SSD_WIDTH, nblk), _state_spec(sshape)],
        out_shape=[jax.ShapeDtypeStruct((n, SSD_WIDTH), F32), jax.ShapeDtypeStruct((batch,) + sshape, F32)],
        scratch_shapes=[pltpu.VMEM((SSD_BC, SSD_WIDTH), F32), pltpu.VMEM((seq, SSD_WIDTH), F32),
                        pltpu.VMEM((TB_SCAN + 2 * HALO, SSD_XBC), F32), pltpu.VMEM((seq, SSD_XBC), F32),
                        pltpu.VMEM((seq, P_DT), F32)],
        compiler_params=_scan_params(),
        name="ssd",
    )(z, xbc, xbc, xbc, dt, s0, cw, cb, dtb, alog, dsk, ng)


def _s5_kernel(*refs, reverse, final, batch):
    if final:
        (u_ref, x0_ref, bblk_ref, a_ref, cblk_ref, yprev_ref, d_ref, wglu_ref, bglu_ref,
         y_ref, xfin_ref, xs_ref, st_ref) = refs
    else:
        u_ref, x0_ref, bblk_ref, a_ref, cblk_ref, y_ref, xfin_ref, xs_ref, st_ref = refs
    i = pl.program_id(0)
    rows = TT_S5 * batch

    @pl.when(i == 0)
    def _():
        st_ref[...] = x0_ref[...]

    u = u_ref[...].reshape(rows, S5_WIDTH)
    xs_ref[...] = _dot(u, bblk_ref[...])
    a_re = jnp.broadcast_to(a_ref[0:1, :], (batch, S5_LANES))
    a_im = jnp.broadcast_to(a_ref[1:2, :], (batch, S5_LANES))

    def step(s, carry):
        x_re, x_im = carry
        t = (TT_S5 - 1 - s) if reverse else s
        row = pl.ds(pl.multiple_of(t * batch, batch), batch)
        n_re = a_re * x_re - a_im * x_im + xs_ref[row, :S5_LANES]
        n_im = a_re * x_im + a_im * x_re + xs_ref[row, S5_LANES:]
        xs_ref[row, :S5_LANES] = n_re
        xs_ref[row, S5_LANES:] = n_im
        return n_re, n_im

    x_re, x_im = lax.fori_loop(0, TT_S5, step, (st_ref[:, :S5_LANES], st_ref[:, S5_LANES:]))
    st_ref[:, :S5_LANES] = x_re
    st_ref[:, S5_LANES:] = x_im
    xfin_ref[:, :S5_LANES] = x_re
    xfin_ref[:, S5_LANES:] = x_im

    y = _dot(xs_ref[...], cblk_ref[...])
    if final:
        y = y + yprev_ref[...].reshape(rows, S5_WIDTH) + u * d_ref[...]
        g = jax.nn.gelu(y)
        y = g * jax.nn.sigmoid(_dot(g, wglu_ref[...]) + bglu_ref[...])
    y_ref[...] = y.reshape(TT_S5, batch, S5_WIDTH)


def _s5_dir(u, x0, bblk, a, cblk, extra, reverse):
    seq, batch, _ = u.shape
    nblk = seq // TT_S5
    final = extra is not None

    def seq_map(i):
        return ((nblk - 1 - i) if reverse else i, 0, 0)

    def const(shape):
        nd = len(shape)
        return pl.BlockSpec(shape, lambda i: (0,) * nd)

    seq_spec = pl.BlockSpec((TT_S5, batch, S5_WIDTH), seq_map)
    in_specs = [seq_spec, const(x0.shape), const(bblk.shape), const(a.shape), const(cblk.shape)]
    args = [u, x0, bblk, a, cblk]
    if final:
        yprev, d, wglu, bglu = extra
        in_specs += [seq_spec, const(d.shape), const(wglu.shape), const(bglu.shape)]
        args += [yprev, d, wglu, bglu]
    return pl.pallas_call(
        functools.partial(_s5_kernel, reverse=reverse, final=final, batch=batch),
        grid=(nblk,),
        in_specs=in_specs,
        out_specs=[seq_spec, const(x0.shape)],
        out_shape=[jax.ShapeDtypeStruct(u.shape, F32), jax.ShapeDtypeStruct(x0.shape, F32)],
        scratch_shapes=[pltpu.VMEM((TT_S5 * batch, 2 * S5_LANES), F32), pltpu.VMEM(x0.shape, F32)],
        compiler_params=pltpu.CompilerParams(dimension_semantics=("arbitrary",), vmem_limit_bytes=VMEM_LIMIT),
        name="s5_bwd" if reverse else "s5_fwd",
    )(*args)


def _s5(u_tm, x0, prm):
    y_b, xf_b = _s5_dir(u_tm, x0[1], prm["bblk"][1], prm["a"][1], prm["cblk"], None, True)
    y, xf_f = _s5_dir(u_tm, x0[0], prm["bblk"][0], prm["a"][0], prm["cblk"],
                      (y_b, prm["d"], prm["wglu"], prm["bglu"]), False)
    return y, jnp.stack([xf_f, xf_b])


def _s5_params(a_re, a_im, log_dt, b_re, b_im, c_re, c_im, d, w_glu, b_glu):
    dt = jnp.exp(log_dt)[..., None]
    mag = jnp.exp(dt * a_re)
    ab_re = mag * jnp.cos(dt * a_im)
    ab_im = mag * jnp.sin(dt * a_im)
    den = a_re * a_re + a_im * a_im
    num_re = ab_re - 1.0
    num_im = ab_im
    f_re = (num_re * a_re + num_im * a_im) / den
    f_im = (num_im * a_re - num_re * a_im) / den
    bb_re = f_re[..., None] * b_re - f_im[..., None] * b_im
    bb_im = f_re[..., None] * b_im + f_im[..., None] * b_re
    eye = jnp.eye(S5_GROUPS, dtype=F32)

    def in_block(bb):
        return jnp.einsum("dgph,gk->dghkp", bb, eye).reshape(2, S5_WIDTH, S5_LANES)

    def out_block(c):
        return jnp.einsum("ghp,gk->gpkh", c, eye).reshape(S5_LANES, S5_WIDTH)

    return {
        "a": jnp.stack([ab_re.reshape(2, S5_LANES), ab_im.reshape(2, S5_LANES)], axis=1),
        "bblk": jnp.concatenate([in_block(bb_re), in_block(bb_im)], axis=-1).astype(BF16),
        "cblk": jnp.concatenate([out_block(c_re), -out_block(c_im)], axis=0).astype(BF16),
        "d": d.reshape(1, S5_WIDTH),
        "wglu": w_glu.astype(BF16),
        "bglu": b_glu.reshape(1, S5_WIDTH),
    }


def _pad_heads(w, heads, width, padded):
    lead = w.shape[:-1]
    w = w.reshape(lead + (heads, width))
    w = jnp.pad(w, [(0, 0)] * len(lead) + [(0, 0), (0, padded - width)])
    return w.reshape(lead + (heads * padded,))


def _pad_last(w, width):
    return jnp.pad(w, [(0, 0)] * (w.ndim - 1) + [(0, width - w.shape[-1])])


def _pack_w_in(w_in):
    pts = []
    acc = 0
    for s in IN_SPLITS:
        pts.append((acc, acc + s))
        acc += s
    q, k, v, r, glr, s5, z, xbc, dt = (w_in[:, a:b] for a, b in pts)
    return jnp.concatenate([
        _pad_heads(q, GLA_HEADS, GLA_DK, GLA_DKP), _pad_heads(k, GLA_HEADS, GLA_DK, GLA_DKP), v, r,
        _pad_last(glr, P_GLR), s5, z, xbc, _pad_last(dt, P_DT)], axis=-1).astype(BF16)


def _pack_gla_gate(w_gate, b_gate):
    w = _pad_heads(w_gate, GLA_HEADS, GLA_DK, GLA_DKP)
    wp = jnp.zeros((2, P_GLR, GLA_QKP), F32)
    for d in range(2):
        wp = wp.at[d, d * GLA_RANK:(d + 1) * GLA_RANK].set(w[d])
    return wp.astype(BF16), _pad_heads(b_gate, GLA_HEADS, GLA_DK, GLA_DKP).reshape(2, 1, GLA_QKP)


def _to_col_major(t, batch):
    n, ch = t.shape
    rows = n // batch // GRID_W
    return t.reshape(batch, rows, GRID_W, ch).transpose(0, 2, 1, 3).reshape(n, ch)


def _from_col_major(t, batch):
    n, ch = t.shape
    rows = n // batch // GRID_W
    return t.reshape(batch, GRID_W, rows, ch).transpose(0, 2, 1, 3).reshape(n, ch)


def _to_time_major(t, batch):
    n, ch = t.shape
    return t.reshape(batch, n // batch, ch).transpose(1, 0, 2)


def _from_time_major(t):
    seq, batch, ch = t.shape
    return t.transpose(1, 0, 2).reshape(seq * batch, ch)


def kernel(x, c, ctx, c_ctx, ada_w, ada_b, norm_g, w_in, w_out, ff_w_gate, ff_w_up, ff_w_down, gla_w_gate, gla_b_gate, gla_norm_g, s5_a_re, s5_a_im, s5_log_dt, s5_b_re, s5_b_im, s5_c_re, s5_c_im, s5_d, s5_w_glu, s5_b_glu, ssd_conv_w, ssd_conv_b, ssd_dt_bias, ssd_a_log, ssd_d, ssd_norm_g, final_norm_g):
    batch, seq, _ = x.shape
    ctx_len = ctx.shape[1]
    depth = ada_w.shape[0]
    assert seq == GRID_W * CHUNK and ctx_len % TB_SCAN == 0 and seq % TM_TOKENS == 0
    assert batch == SUBLANE and (batch * ctx_len) % TM_TOKENS == 0

    mod_rows = 2 * SUBLANE
    cc = jnp.zeros((mod_rows, D_MODEL), F32).at[:batch].set(c).at[batch].set(c_ctx)
    mods = _ada(cc, ada_w, ada_b).reshape(depth, mod_rows, N_MOD, D_MODEL)
    lat_mod = _mod_spec(seq // TM_TOKENS, None)
    ctx_mod = _mod_spec(None, batch)

    h = x.reshape(batch * seq, D_MODEL)
    hc = ctx.reshape(batch * ctx_len, D_MODEL)
    fg = final_norm_g.reshape(1, D_MODEL)

    for i in range(depth):
        ctx_out = i < depth - 1
        mod = mods[i]
        ng = norm_g[i]
        wg = ff_w_gate[i].astype(BF16)
        wu = ff_w_up[i].astype(BF16)
        wd = ff_w_down[i].astype(BF16)
        w_all = _pack_w_in(w_in[i])
        wo = w_out[i].astype(BF16)
        wog, wos, woc = wo[:GLA_WIDTH], wo[GLA_WIDTH:GLA_WIDTH + S5_WIDTH], wo[GLA_WIDTH + S5_WIDTH:]
        gla_wg, gla_bg = _pack_gla_gate(gla_w_gate[i], gla_b_gate[i])
        gla_ng = gla_norm_g[i].reshape(1, GLA_WIDTH)
        s5p = _s5_params(s5_a_re[i], s5_a_im[i], s5_log_dt[i], s5_b_re[i], s5_b_im[i], s5_c_re[i], s5_c_im[i],
                         s5_d[i], s5_w_glu[i], s5_b_glu[i])
        cw = jnp.pad(ssd_conv_w[i], ((0, SUBLANE - SSD_CONV), (0, 0)))
        cb = ssd_conv_b[i].reshape(1, SSD_XBC)
        dtb = _pad_last(ssd_dt_bias[i].reshape(1, 2 * SSD_HEADS), P_DT)
        alog = _pad_last(ssd_a_log[i].reshape(1, 2 * SSD_HEADS), P_DT)
        dsk = jnp.repeat(ssd_d[i], SSD_HEADDIM).reshape(1, SSD_WIDTH)
        ssd_ng = ssd_norm_g[i].reshape(1, SSD_WIDTH)

        h = _ffn1(h, mod, lat_mod, ng, wg[0], wu[0], wd[0])
        hc = _ffn1(hc, mod, ctx_mod, ng, wg[0], wu[0], wd[0])

        qk_l, v_l, r_l, glr_l, s5_l, z_l, xbc_l, dt_l = _inproj(h, mod, lat_mod, ng, w_all)
        qk_c, v_c, r_c, glr_c, s5_c, z_c, xbc_c, dt_c = _inproj(hc, mod, ctx_mod, ng, w_all)

        gla_s0 = jnp.zeros((batch, 2, GLA_WIDTH, GLA_QKP), F32)
        yg_c, gla_s = _gla(qk_c, v_c, r_c, glr_c, gla_s0, gla_wg, gla_bg, gla_ng, batch)
        yg_l, _ = _gla(qk_l, v_l, r_l, glr_l, gla_s, gla_wg, gla_bg, gla_ng, batch)

        s5_x0 = jnp.zeros((2, batch, 2 * S5_LANES), F32)
        ys_c, s5_x = _s5(_to_time_major(s5_c, batch), s5_x0, s5p)
        ys_l, _ = _s5(_to_time_major(s5_l, batch), s5_x, s5p)
        ys_l = _from_time_major(ys_l)

        ssd_s0 = jnp.zeros((batch, 2, SSD_BC, SSD_WIDTH), F32)
        yc_c, ssd_s = _ssd(z_c, xbc_c, dt_c, ssd_s0, cw, cb, dtb, alog, dsk, ssd_ng, batch)
        yc_l, _ = _ssd(_to_col_major(z_l, batch), _to_col_major(xbc_l, batch), _to_col_major(dt_l, batch),
                       ssd_s, cw, cb, dtb, alog, dsk, ssd_ng, batch)
        yc_l = _from_col_major(yc_l, batch)

        h = _mix_ffn2(h, yg_l, ys_l, yc_l, mod, lat_mod, ng, wog, wos, woc, wg[1], wu[1], wd[1], fg,
                      final=not ctx_out)
        if ctx_out:
            hc = _mix_ffn2(hc, yg_c, _from_time_major(ys_c), yc_c, mod, ctx_mod, ng, wog, wos, woc,
                           wg[1], wu[1], wd[1], fg, final=False)
    return h.reshape(batch, seq, D_MODEL)
```

```python
import functools
import math

import jax
import jax.numpy as jnp
from jax import lax
from jax.experimental import pallas as pl
from jax.experimental.pallas import tpu as pltpu

F32 = jnp.float32
BF16 = jnp.bfloat16
HI = lax.Precision.HIGHEST

D_MODEL = 1024
GRID_W = 64
N_MOD = 9
D_FF = 2816
EPS = 1e-6

GLA_HEADS = 4
GLA_DK = 48
GLA_DKP = 64
GLA_DV = 96
GLA_RANK = 16
GLA_TAU = 16.0
GLA_QK = GLA_HEADS * GLA_DK
GLA_QKP = GLA_HEADS * GLA_DKP
GLA_WIDTH = GLA_HEADS * GLA_DV

S5_WIDTH = 256
S5_GROUP = 16
S5_GROUPS = 16
S5_STATE = 64
S5_LANES = S5_GROUPS * S5_STATE
S5_STREAMS = 2
S5_HALF = S5_LANES // S5_STREAMS
S5_PARTS = 4

SSD_HEADS = 6
SSD_HEADDIM = 64
SSD_GROUPS = 2
SSD_REP = SSD_HEADS // SSD_GROUPS
SSD_STATE = 128
SSD_CONV = 5
SSD_WIDTH = SSD_HEADS * SSD_HEADDIM
SSD_BC = SSD_GROUPS * SSD_STATE
SSD_XBC = SSD_WIDTH + 2 * SSD_BC

MIX_WIDTH = GLA_WIDTH + S5_WIDTH + SSD_WIDTH
IN_SPLITS = (GLA_QK, GLA_QK, GLA_WIDTH, GLA_WIDTH, 2 * GLA_RANK, S5_WIDTH, SSD_WIDTH, SSD_XBC, 2 * SSD_HEADS)

CHUNK = 64
LANE = 128
SUBLANE = 8
HALO = SUBLANE

P_QK = 2 * GLA_QKP
P_GLR = LANE
P_DT = LANE
P_TOTAL = P_QK + 2 * GLA_WIDTH + P_GLR + S5_WIDTH + SSD_WIDTH + SSD_XBC + P_DT

TM_TOKENS = 512
TB_SCAN = 256
SCAN_BATCH = 2
TT_S5 = 128
VMEM_LIMIT = 56 * 1024 * 1024


def _silu(x):
    return x * jax.nn.sigmoid(x)


def _rms(x, g):
    return x * lax.rsqrt(jnp.mean(x * x, axis=-1, keepdims=True) + EPS) * g


def _dot(a, b):
    return jnp.dot(a.astype(BF16), b.astype(BF16), preferred_element_type=F32)


def _dot_nt(a, b):
    return lax.dot_general(a.astype(BF16), b.astype(BF16), (((1,), (1,)), ((), ())), preferred_element_type=F32)


def _dot_tn(a, b):
    return lax.dot_general(a.astype(BF16), b.astype(BF16), (((0,), (0,)), ((), ())), preferred_element_type=F32)


def _dot_hi(a, b):
    return jnp.dot(a, b, precision=HI, preferred_element_type=F32)


def _split3(x):
    hi = x.astype(BF16)
    rest = x - hi.astype(F32)
    mid = rest.astype(BF16)
    lo = (rest - mid.astype(F32)).astype(BF16)
    return hi, mid, lo


def _dot_exact_rhs(x, m01):
    return jnp.dot(jnp.concatenate(_split3(x), axis=1), jnp.concatenate([m01] * 3, axis=0),
                   preferred_element_type=F32)


def _dot_exact_lhs(m01, x):
    return jnp.dot(jnp.concatenate([m01] * 3, axis=1), jnp.concatenate(_split3(x), axis=0),
                   preferred_element_type=F32)


_DONE = object()


def _interleave(*streams):
    live = list(streams)
    while live:
        for stream in tuple(live):
            if next(stream, _DONE) is _DONE:
                live.remove(stream)


def _iota(shape, dim):
    return lax.broadcasted_iota(jnp.int32, shape, dim)


def _block_id(shape, dim, width):
    idx = _iota(shape, dim)
    out = jnp.zeros(shape, F32)
    for k in range(1, -(-shape[dim] // width)):
        out = out + jnp.where(idx >= k * width, 1.0, 0.0)
    return out


def _chunk_tri(n, lower):
    i = _iota((n, n), 0)
    j = _iota((n, n), 1)
    same_chunk = (i & -CHUNK) == (j & -CHUNK)
    return jnp.where(same_chunk & ((j <= i) if lower else (j >= i)), 1.0, 0.0).astype(BF16)


def _ada_kernel(c_ref, w_ref, b_ref, o_ref):
    o_ref[0] = _dot_hi(_silu(c_ref[...]), w_ref[0]) + b_ref[0]


def _ada(cc, ada_w, ada_b):
    depth = ada_w.shape[0]
    rows = cc.shape[0]
    tn = D_MODEL
    return pl.pallas_call(
        _ada_kernel,
        grid=(depth, N_MOD * D_MODEL // tn),
        in_specs=[pl.BlockSpec((rows, D_MODEL), lambda l, n: (0, 0)),
                  pl.BlockSpec((1, D_MODEL, tn), lambda l, n: (l, 0, n)),
                  pl.BlockSpec((1, 1, tn), lambda l, n: (l, 0, n))],
        out_specs=pl.BlockSpec((1, rows, tn), lambda l, n: (l, 0, n)),
        out_shape=jax.ShapeDtypeStruct((depth, rows, N_MOD * D_MODEL), F32),
        name="ada_mod",
    )(cc, ada_w, ada_b.reshape(depth, 1, N_MOD * D_MODEL))


def _ffn_body(x, m, g, wg_ref, wu_ref, wd_ref, base):
    u = (_rms(x, g) * (1.0 + m[base + 1:base + 2]) + m[base:base + 1]).astype(BF16)
    gate = jnp.dot(u, wg_ref[...], preferred_element_type=F32)
    up = jnp.dot(u, wu_ref[...], preferred_element_type=F32)
    act = (_silu(gate) * up).astype(BF16)
    y = jnp.dot(act, wd_ref[...], preferred_element_type=F32)
    return x + 0.5 * m[base + 2:base + 3] * y


def _ffn1_kernel(h_ref, mod_ref, ng_ref, wg_ref, wu_ref, wd_ref, o_ref):
    o_ref[...] = _ffn_body(h_ref[...], mod_ref[0], ng_ref[0:1], wg_ref, wu_ref, wd_ref, 0)


def _mix_ffn2_kernel(h_ref, yg_ref, ys_ref, yc_ref, mod_ref, ng_ref, wog_ref, wos_ref, woc_ref,
                     wg_ref, wu_ref, wd_ref, fg_ref, o_ref, *, final):
    m = mod_ref[0]
    mix = (jnp.dot(yg_ref[...].astype(BF16), wog_ref[...], preferred_element_type=F32)
           + jnp.dot(ys_ref[...].astype(BF16), wos_ref[...], preferred_element_type=F32)
           + jnp.dot(yc_ref[...].astype(BF16), woc_ref[...], preferred_element_type=F32))
    x = h_ref[...] + m[5:6] * mix
    x = _ffn_body(x, m, ng_ref[2:3], wg_ref, wu_ref, wd_ref, 6)
    if final:
        x = _rms(x, fg_ref[...])
    o_ref[...] = x


def _const_spec(shape):
    nd = len(shape)
    return pl.BlockSpec(shape, lambda t: (0,) * nd, pipeline_mode=pl.Buffered(1))


def _tile_spec(width):
    return pl.BlockSpec((TM_TOKENS, width), lambda t: (t, 0))


def _mod_spec(tiles_per_row, fixed_row):
    if fixed_row is None:
        return pl.BlockSpec((1, N_MOD, D_MODEL), lambda t: (t // tiles_per_row, 0, 0))
    return pl.BlockSpec((1, N_MOD, D_MODEL), lambda t: (fixed_row, 0, 0))


def _dense_params():
    return pltpu.CompilerParams(dimension_semantics=("arbitrary",), vmem_limit_bytes=VMEM_LIMIT)


def _ffn1(h, mod, mod_spec, ng, wg, wu, wd):
    n = h.shape[0]
    return pl.pallas_call(
        _ffn1_kernel,
        grid=(n // TM_TOKENS,),
        in_specs=[_tile_spec(D_MODEL), mod_spec, _const_spec(ng.shape),
                  _const_spec(wg.shape), _const_spec(wu.shape), _const_spec(wd.shape)],
        out_specs=_tile_spec(D_MODEL),
        out_shape=jax.ShapeDtypeStruct(h.shape, F32),
        compiler_params=_dense_params(),
        name="ffn1",
    )(h, mod, ng, wg, wu, wd)


def _mix_ffn2(h, yg, ys, yc, mod, mod_spec, ng, wog, wos, woc, wg, wu, wd, fg, final):
    n = h.shape[0]
    return pl.pallas_call(
        functools.partial(_mix_ffn2_kernel, final=final),
        grid=(n // TM_TOKENS,),
        in_specs=[_tile_spec(D_MODEL), _tile_spec(GLA_WIDTH), _tile_spec(S5_WIDTH), _tile_spec(SSD_WIDTH),
                  mod_spec, _const_spec(ng.shape), _const_spec(wog.shape), _const_spec(wos.shape),
                  _const_spec(woc.shape), _const_spec(wg.shape), _const_spec(wu.shape), _const_spec(wd.shape),
                  _const_spec(fg.shape)],
        out_specs=_tile_spec(D_MODEL),
        out_shape=jax.ShapeDtypeStruct(h.shape, F32),
        compiler_params=_dense_params(),
        name="mix_ffn2",
    )(h, yg, ys, yc, mod, ng, wog, wos, woc, wg, wu, wd, fg)


_P_WIDTHS = (P_QK, GLA_WIDTH, GLA_WIDTH, P_GLR, S5_WIDTH, SSD_WIDTH, SSD_XBC, P_DT)


def _inproj_kernel(h_ref, mod_ref, ng_ref, w_ref, *out_refs):
    m = mod_ref[0]
    u = _rms(h_ref[...], ng_ref[1:2]) * (1.0 + m[4:5]) + m[3:4]
    p = jnp.dot(u.astype(BF16), w_ref[...], preferred_element_type=F32)
    off = 0
    for ref, width in zip(out_refs, _P_WIDTHS):
        ref[...] = p[:, off:off + width]
        off += width


def _inproj(h, mod, mod_spec, ng, w_all):
    n = h.shape[0]
    return pl.pallas_call(
        _inproj_kernel,
        grid=(n // TM_TOKENS,),
        in_specs=[_tile_spec(D_MODEL), mod_spec, _const_spec(ng.shape), _const_spec(w_all.shape)],
        out_specs=[_tile_spec(w) for w in _P_WIDTHS],
        out_shape=[jax.ShapeDtypeStruct((n, w), F32) for w in _P_WIDTHS],
        compiler_params=_dense_params(),
        name="inproj",
    )(h, mod, ng, w_all)


def _scan_blk(ph, j, nblk):
    return jnp.where(ph == 0, nblk - 1 - j, j)


def _seq_spec(width, nblk):
    return pl.BlockSpec((SCAN_BATCH, TB_SCAN, width), lambda g, ph, j: (g, _scan_blk(ph, j, nblk), 0))


def _phase0_blk(ph, j, nblk):
    return jnp.where(ph == 0, nblk - 1 - j, 0)


def _seq_phase0_spec(width, nblk):
    return pl.BlockSpec((SCAN_BATCH, TB_SCAN, width), lambda g, ph, j: (g, _phase0_blk(ph, j, nblk), 0))


def _seq_phase1_spec(width, nblk):
    return pl.BlockSpec((SCAN_BATCH, TB_SCAN, width), lambda g, ph, j: (g, jnp.where(ph == 0, 0, j), 0))


def _state_spec(shape):
    return pl.BlockSpec((SCAN_BATCH,) + shape, lambda g, ph, j: (g,) + (0,) * len(shape))


def _param_spec(shape):
    nd = len(shape)
    return pl.BlockSpec(shape, lambda b, ph, j: (0,) * nd)


def _scan_params():
    return pltpu.CompilerParams(dimension_semantics=("arbitrary", "arbitrary", "arbitrary"),
                                vmem_limit_bytes=VMEM_LIMIT)


def _gla_kernel(qk_ref, v_ref, r_ref, glr_ref, s0_ref, wg_ref, bg_ref, ng_ref,
                y_ref, sfin_ref, st_ref, ob_ref, qkb_ref, *, nblk):
    ph = pl.program_id(1)
    j = pl.program_id(2)
    nch = TB_SCAN // CHUNK
    scale = GLA_DK ** -0.5

    lane_k = _block_id((1, GLA_QKP), 1, GLA_DKP)
    lane_v = _block_id((1, GLA_WIDTH), 1, GLA_DV)
    kmask16 = [(lane_k == h).astype(BF16) for h in range(GLA_HEADS)]
    vmask16 = [(lane_v == h).astype(BF16) for h in range(GLA_HEADS)]
    blockmask = (_block_id((GLA_WIDTH, GLA_QKP), 0, GLA_DV)
                 == _block_id((GLA_WIDTH, GLA_QKP), 1, GLA_DKP)).astype(F32)

    def decay_cum(s, d):
        g = _dot(glr_ref[s], wg_ref[d]) + bg_ref[d]
        return _dot_exact_lhs(_chunk_tri(TB_SCAN, lower=(d == 0)), jax.nn.log_sigmoid(g) * (1.0 / GLA_TAU))

    def rows(x, c):
        return x[c * CHUNK:(c + 1) * CHUNK]

    def per_head_rows(x, masks):
        return jnp.concatenate([x * m for m in masks], axis=0)

    def backward_sweep(s):
        base = pl.multiple_of((nblk - 1 - j) * TB_SCAN, TB_SCAN)
        cum = decay_cum(s, 1)
        yield
        q_dec = (qk_ref[s, :, :GLA_QKP] * scale * jnp.exp(cum)).astype(BF16)
        k_inv = qk_ref[s, :, GLA_QKP:] * jnp.exp(-cum)
        v = v_ref[s]
        qkb_ref[s, pl.ds(base, TB_SCAN), :GLA_QKP] = q_dec
        qkb_ref[s, pl.ds(base, TB_SCAN), GLA_QKP:] = k_inv.astype(BF16)
        state = st_ref[s]
        yield
        for c in reversed(range(nch)):
            ob_ref[s, pl.ds(base + c * CHUNK, CHUNK), :] = _dot_nt(rows(q_dec, c), state)
            yield
            decay = jnp.exp(rows(cum, c)[0:1, :])
            state = state * decay + _dot_tn(rows(v, c), rows(k_inv, c) * decay) * blockmask
            yield
        st_ref[s] = state

    def forward_sweep(s):
        base = pl.multiple_of(j * TB_SCAN, TB_SCAN)
        row_i = _iota((CHUNK, GLA_HEADS * CHUNK), 0)
        lane_j = _iota((CHUNK, GLA_HEADS * CHUNK), 1) & (CHUNK - 1)
        same_head = (_block_id((GLA_WIDTH, GLA_WIDTH), 0, GLA_DV)
                     == _block_id((GLA_WIDTH, GLA_WIDTH), 1, GLA_DV)).astype(BF16)
        cum_f = decay_cum(s, 0)
        yield
        q_f = qk_ref[s, :, :GLA_QKP] * scale * jnp.exp(cum_f)
        k_f = qk_ref[s, :, GLA_QKP:] * jnp.exp(-cum_f)
        k_f16 = k_f.astype(BF16)
        v = v_ref[s]
        v16 = v.astype(BF16)
        q_b = qkb_ref[s, pl.ds(base, TB_SCAN), :GLA_QKP]
        k_b = qkb_ref[s, pl.ds(base, TB_SCAN), GLA_QKP:]
        state = st_ref[s]
        outs = []
        yield
        for c in range(nch):
            a_f = _dot_nt(rows(q_f, c), per_head_rows(rows(k_f16, c), kmask16))
            a_b = _dot_nt(rows(q_b, c), per_head_rows(rows(k_b, c), kmask16))
            yield
            scores = jnp.where(lane_j <= row_i, a_f, 0.0) + jnp.where(lane_j >= row_i, a_b, 0.0)
            o = _dot(scores, per_head_rows(rows(v16, c), vmask16)) + _dot_nt(rows(q_f, c), state)
            outs.append(o)
            yield
            decay = jnp.exp(rows(cum_f, c)[CHUNK - 1:CHUNK, :])
            state = state * decay + _dot_tn(rows(v, c), rows(k_f, c) * decay) * blockmask
            yield
        st_ref[s] = state
        o = jnp.concatenate(outs, axis=0) + ob_ref[s, pl.ds(base, TB_SCAN), :]
        mean_sq = _dot_exact_rhs(o * o, same_head) * (1.0 / GLA_DV)
        yield
        y_ref[s] = o * lax.rsqrt(mean_sq + EPS) * ng_ref[...] * _silu(r_ref[s])

    @pl.when(j == 0)
    def _():
        st_ref[...] = jnp.where(ph == 0, s0_ref[:, 1], s0_ref[:, 0])

    @pl.when(ph == 0)
    def _():
        _interleave(*(backward_sweep(s) for s in range(SCAN_BATCH)))

    @pl.when(ph == 1)
    def _():
        _interleave(*(forward_sweep(s) for s in range(SCAN_BATCH)))

    @pl.when(j == nblk - 1)
    def _():
        @pl.when(ph == 0)
        def _():
            sfin_ref[:, 1] = st_ref[...]

        @pl.when(ph == 1)
        def _():
            sfin_ref[:, 0] = st_ref[...]


def _gla(qk, v, r, glr, s0, wg, bg, ng):
    batch, seq, _ = qk.shape
    nblk = seq // TB_SCAN
    sshape = (2, GLA_WIDTH, GLA_QKP)
    return pl.pallas_call(
        functools.partial(_gla_kernel, nblk=nblk),
        grid=(batch // SCAN_BATCH, 2, nblk),
        in_specs=[_seq_spec(P_QK, nblk), _seq_spec(GLA_WIDTH, nblk), _seq_phase1_spec(GLA_WIDTH, nblk),
                  _seq_spec(P_GLR, nblk), _state_spec(sshape),
                  _param_spec(wg.shape), _param_spec(bg.shape), _param_spec(ng.shape)],
        out_specs=[_seq_phase1_spec(GLA_WIDTH, nblk), _state_spec(sshape)],
        out_shape=[jax.ShapeDtypeStruct((batch, seq, GLA_WIDTH), F32),
                   jax.ShapeDtypeStruct((batch,) + sshape, F32)],
        scratch_shapes=[pltpu.VMEM((SCAN_BATCH, GLA_WIDTH, GLA_QKP), F32),
                        pltpu.VMEM((SCAN_BATCH, seq, GLA_WIDTH), F32),
                        pltpu.VMEM((SCAN_BATCH, seq, P_QK), BF16)],
        compiler_params=_scan_params(),
        name="gla",
    )(qk, v, r, glr, s0, wg, bg, ng)


def _ssd_kernel(z_ref, x_ref, xp_ref, xn_ref, dt_ref, s0_ref, cw_ref, cb_ref, dtb_ref, alog_ref, dsk_ref, ng_ref,
                y_ref, sfin_ref, st_ref, yb_ref, ext_ref, xs_ref, bc_ref, cumb_ref, *, nblk):
    ph = pl.program_id(1)
    j = pl.program_id(2)
    blk = _scan_blk(ph, j, nblk)
    base = pl.multiple_of(blk * TB_SCAN, TB_SCAN)
    nch = TB_SCAN // CHUNK

    lane_h = _block_id((1, SSD_WIDTH), 1, SSD_HEADDIM)
    lane_g = _block_id((1, SSD_WIDTH), 1, SSD_HEADDIM * SSD_REP)
    lane_n = _block_id((1, SSD_BC), 1, SSD_STATE)
    head_lanes = [(lane_h == h).astype(BF16) for h in range(SSD_HEADS)]
    group_lanes = [(lane_g == g).astype(BF16) for g in range(SSD_GROUPS)]
    group_states = [(lane_n == g).astype(BF16) for g in range(SSD_GROUPS)]

    def expand(d):
        return jnp.where(_iota((LANE, SSD_WIDTH), 0) == d * SSD_HEADS + _block_id((LANE, SSD_WIDTH), 1, SSD_HEADDIM),
                         1.0, 0.0).astype(BF16)

    def rows(x, c):
        return x[c * CHUNK:(c + 1) * CHUNK]

    def conv_silu(s):
        ext_ref[s, 0:HALO, :] = jnp.where(blk > 0, xp_ref[s], 0.0)
        ext_ref[s, HALO:HALO + TB_SCAN, :] = x_ref[s]
        ext_ref[s, HALO + TB_SCAN:, :] = jnp.where(blk < nblk - 1, xn_ref[s], 0.0)
        pad = (SSD_CONV - 1) // 2
        acc = cb_ref[...]
        for t in range(SSD_CONV):
            acc = acc + ext_ref[s, pl.ds(HALO - pad + t, TB_SCAN), :] * cw_ref[t:t + 1, :]
        return _silu(acc)

    def decay_inputs(s):
        dt_c = jax.nn.softplus(dt_ref[s] + dtb_ref[...])
        return dt_c, dt_c * -jnp.exp(alog_ref[...])

    def expanded(d, cum_c, dt_c):
        both = _dot_exact_rhs(jnp.concatenate([cum_c, dt_c], axis=0), expand(d))
        return both[:TB_SCAN], both[TB_SCAN:]

    def scan_terms(d, cum_e, dt_e):
        cum = cum_e.reshape(nch, CHUNK, SSD_WIDTH)
        last = cum[:, CHUNK - 1:CHUNK, :] if d == 0 else cum[:, 0:1, :]
        weight = jnp.exp(last - cum).reshape(TB_SCAN, SSD_WIDTH) * dt_e
        return jnp.exp(cum_e), weight, jnp.exp(last)

    def advance(state, decay, bm_c, xw_c):
        xw16 = xw_c.astype(BF16)
        contrib = [_dot_tn(bm_c[:, g * SSD_STATE:(g + 1) * SSD_STATE], xw16 * group_lanes[g])
                   for g in range(SSD_GROUPS)]
        return state * decay + jnp.concatenate(contrib, axis=0)

    def backward_sweep(s):
        xbc = conv_silu(s)
        yield
        dt_c, la_c = decay_inputs(s)
        cum_c = _dot_exact_lhs(_chunk_tri(TB_SCAN, lower=False), la_c)
        yield
        xs = xbc[:, :SSD_WIDTH]
        bc16 = xbc[:, SSD_WIDTH:].astype(BF16)
        xs_ref[s, pl.ds(base, TB_SCAN), :] = xs
        bc_ref[s, pl.ds(base, TB_SCAN), :] = bc16
        cumb_ref[s, pl.ds(base, TB_SCAN), :] = cum_c
        bm, cm = bc16[:, :SSD_BC], bc16[:, SSD_BC:]
        cum_e, dt_e = expanded(1, cum_c, dt_c)
        yield
        grow, weight, decay = scan_terms(1, cum_e, dt_e)
        xw = xs * weight
        state = st_ref[s]
        outs = [None] * nch
        yield
        for c in reversed(range(nch)):
            outs[c] = _dot(rows(cm, c), state)
            state = advance(state, decay[c], rows(bm, c), rows(xw, c))
            yield
        st_ref[s] = state
        yb_ref[s, pl.ds(base, TB_SCAN), :] = jnp.concatenate(outs, axis=0) * grow

    def forward_sweep(s):
        xs = xs_ref[s, pl.ds(base, TB_SCAN), :]
        bm = bc_ref[s, pl.ds(base, TB_SCAN), :SSD_BC]
        cm = bc_ref[s, pl.ds(base, TB_SCAN), SSD_BC:]
        dt_c, la_c = decay_inputs(s)
        cum_cf = _dot_exact_lhs(_chunk_tri(TB_SCAN, lower=True), la_c)
        yield
        cum_cb = cumb_ref[s, pl.ds(base, TB_SCAN), :]
        cum_f, dt_f = expanded(0, cum_cf, dt_c)
        yield
        cum_b, dt_b = expanded(1, cum_cb, dt_c)
        yield
        grow, weight, decay = scan_terms(0, cum_f, dt_f)
        xw = xs * weight
        ci = _iota((CHUNK, SSD_WIDTH), 0)
        cj = _iota((CHUNK, SSD_WIDTH), 1) & (CHUNK - 1)
        diag = (_iota((TB_SCAN, SSD_WIDTH), 1) & (CHUNK - 1)) == (_iota((TB_SCAN, SSD_WIDTH), 0) & (CHUNK - 1))

        def at_j(x):
            return jnp.sum(jnp.where(diag, x, 0.0).reshape(nch, CHUNK, SSD_WIDTH), axis=1, keepdims=True)

        cum_row_f, cum_row_b, dt_row_f, dt_row_b = at_j(cum_f), at_j(cum_b), at_j(dt_f), at_j(dt_b)
        xs16 = xs.astype(BF16)
        state = st_ref[s]
        outs = []
        yield
        for c in range(nch):
            bm_group = [rows(bm, c) * group_states[g] for g in range(SSD_GROUPS)]
            bm_heads = jnp.concatenate([bm_group[h // SSD_REP] for h in range(SSD_HEADS)], axis=0)
            scores = _dot_nt(rows(cm, c), bm_heads)
            yield
            l_f = jnp.exp(jnp.where(cj <= ci, rows(cum_f, c) - cum_row_f[c], -jnp.inf))
            l_b = jnp.exp(jnp.where(cj >= ci, rows(cum_b, c) - cum_row_b[c], -jnp.inf))
            mat = scores * (l_f * dt_row_f[c] + l_b * dt_row_b[c])
            x_heads = jnp.concatenate([rows(xs16, c) * head_lanes[h] for h in range(SSD_HEADS)], axis=0)
            outs.append(_dot(mat, x_heads) + _dot(rows(cm, c), state) * rows(grow, c))
            yield
            state = advance(state, decay[c], rows(bm, c), rows(xw, c))
            yield
        st_ref[s] = state
        y = jnp.concatenate(outs, axis=0) + yb_ref[s, pl.ds(base, TB_SCAN), :] + xs * dsk_ref[...]
        y_ref[s] = _rms(y * _silu(z_ref[s]), ng_ref[...])

    @pl.when(j == 0)
    def _():
        st_ref[...] = jnp.where(ph == 0, s0_ref[:, 1], s0_ref[:, 0])

    @pl.when(ph == 0)
    def _():
        _interleave(*(backward_sweep(s) for s in range(SCAN_BATCH)))

    @pl.when(ph == 1)
    def _():
        _interleave(*(forward_sweep(s) for s in range(SCAN_BATCH)))

    @pl.when(j == nblk - 1)
    def _():
        @pl.when(ph == 0)
        def _():
            sfin_ref[:, 1] = st_ref[...]

        @pl.when(ph == 1)
        def _():
            sfin_ref[:, 0] = st_ref[...]


def _ssd(z, xbc, dt, s0, cw, cb, dtb, alog, dsk, ng):
    batch, seq, _ = z.shape
    nblk = seq // TB_SCAN
    tiles = TB_SCAN // HALO
    sshape = (2, SSD_BC, SSD_WIDTH)

    def prev_map(g, ph, j):
        return (g, jnp.maximum(_phase0_blk(ph, j, nblk) * tiles - 1, 0), 0)

    def next_map(g, ph, j):
        return (g, jnp.minimum((_phase0_blk(ph, j, nblk) + 1) * tiles, seq // HALO - 1), 0)

    halo = (SCAN_BATCH, HALO, SSD_XBC)
    return pl.pallas_call(
        functools.partial(_ssd_kernel, nblk=nblk),
        grid=(batch // SCAN_BATCH, 2, nblk),
        in_specs=[_seq_phase1_spec(SSD_WIDTH, nblk), _seq_phase0_spec(SSD_XBC, nblk),
                  pl.BlockSpec(halo, prev_map), pl.BlockSpec(halo, next_map),
                  _seq_spec(P_DT, nblk), _state_spec(sshape),
                  _param_spec(cw.shape), _param_spec(cb.shape), _param_spec(dtb.shape), _param_spec(alog.shape),
                  _param_spec(dsk.shape), _param_spec(ng.shape)],
        out_specs=[_seq_phase1_spec(SSD_WIDTH, nblk), _state_spec(sshape)],
        out_shape=[jax.ShapeDtypeStruct((batch, seq, SSD_WIDTH), F32),
                   jax.ShapeDtypeStruct((batch,) + sshape, F32)],
        scratch_shapes=[pltpu.VMEM((SCAN_BATCH, SSD_BC, SSD_WIDTH), F32),
                        pltpu.VMEM((SCAN_BATCH, seq, SSD_WIDTH), F32),
                        pltpu.VMEM((SCAN_BATCH, TB_SCAN + 2 * HALO, SSD_XBC), F32),
                        pltpu.VMEM((SCAN_BATCH, seq, SSD_WIDTH), F32),
                        pltpu.VMEM((SCAN_BATCH, seq, 2 * SSD_BC), BF16),
                        pltpu.VMEM((SCAN_BATCH, seq, P_DT), F32)],
        compiler_params=_scan_params(),
        name="ssd",
    )(z, xbc, xbc, xbc, dt, s0, cw, cb, dtb, alog, dsk, ng)


def _s5_kernel(*refs, reverse, final, batch):
    if final:
        (u_ref, x0_ref, bblk_ref, a_ref, cblk_ref, yprev_ref, d_ref, wglu_ref, bglu_ref,
         y_ref, xfin_ref, xs_ref, st_ref) = refs
    else:
        u_ref, x0_ref, bblk_ref, a_ref, cblk_ref, y_ref, xfin_ref, xs_ref, st_ref = refs
    i = pl.program_id(0)
    rows = TT_S5 * batch

    @pl.when(i == 0)
    def _():
        st_ref[...] = x0_ref[...]

    u = u_ref[...].reshape(rows, S5_WIDTH)
    u16 = u.astype(BF16)
    piece = 2 * S5_HALF // S5_PARTS
    readout = [None] * S5_STREAMS

    def stream(h):
        lo = h * 2 * S5_HALF
        re = slice(lo, lo + S5_HALF)
        im = slice(lo + S5_HALF, lo + 2 * S5_HALF)
        for p in range(S5_PARTS):
            cols = slice(lo + p * piece, lo + (p + 1) * piece)
            xs_ref[:, cols] = jnp.dot(u16, bblk_ref[:, cols], preferred_element_type=F32)
            yield
        a_re = jnp.broadcast_to(a_ref[0:1, h * S5_HALF:(h + 1) * S5_HALF], (batch, S5_HALF))
        a_im = jnp.broadcast_to(a_ref[1:2, h * S5_HALF:(h + 1) * S5_HALF], (batch, S5_HALF))
        x_re = st_ref[:, re]
        x_im = st_ref[:, im]
        for n in range(TT_S5):
            t = (TT_S5 - 1 - n) if reverse else n
            row = slice(t * batch, (t + 1) * batch)
            x_re, x_im = (a_re * x_re - a_im * x_im + xs_ref[row, re],
                          a_re * x_im + a_im * x_re + xs_ref[row, im])
            xs_ref[row, re] = x_re
            xs_ref[row, im] = x_im
            if (n + 1) % (TT_S5 // S5_PARTS) == 0:
                yield
        st_ref[:, re] = x_re
        st_ref[:, im] = x_im
        xfin_ref[:, re] = x_re
        xfin_ref[:, im] = x_im
        acc = None
        for p in range(S5_PARTS):
            cols = slice(lo + p * piece, lo + (p + 1) * piece)
            part = jnp.dot(xs_ref[:, cols].astype(BF16), cblk_ref[cols, :], preferred_element_type=F32)
            acc = part if acc is None else acc + part
            yield
        readout[h] = acc

    def lagged(body, stages):
        for _ in range(stages):
            yield
        yield from body

    _interleave(*(lagged(stream(h), h * S5_PARTS) for h in range(S5_STREAMS)))
    y = sum(readout[1:], readout[0])
    if final:
        y = y + yprev_ref[...].reshape(rows, S5_WIDTH) + u * d_ref[...]
        g = jax.nn.gelu(y)
        y = g * jax.nn.sigmoid(_dot(g, wglu_ref[...]) + bglu_ref[...])
    y_ref[...] = y.reshape(TT_S5, batch, S5_WIDTH)


def _s5_dir(u, x0, bblk, a, cblk, extra, reverse):
    seq, batch, _ = u.shape
    nblk = seq // TT_S5
    final = extra is not None

    def seq_map(i):
        return ((nblk - 1 - i) if reverse else i, 0, 0)

    def const(shape):
        nd = len(shape)
        return pl.BlockSpec(shape, lambda i: (0,) * nd)

    seq_spec = pl.BlockSpec((TT_S5, batch, S5_WIDTH), seq_map)
    in_specs = [seq_spec, const(x0.shape), const(bblk.shape), const(a.shape), const(cblk.shape)]
    args = [u, x0, bblk, a, cblk]
    if final:
        yprev, d, wglu, bglu = extra
        in_specs += [seq_spec, const(d.shape), const(wglu.shape), const(bglu.shape)]
        args += [yprev, d, wglu, bglu]
    return pl.pallas_call(
        functools.partial(_s5_kernel, reverse=reverse, final=final, batch=batch),
        grid=(nblk,),
        in_specs=in_specs,
        out_specs=[seq_spec, const(x0.shape)],
        out_shape=[jax.ShapeDtypeStruct(u.shape, F32), jax.ShapeDtypeStruct(x0.shape, F32)],
        scratch_shapes=[pltpu.VMEM((TT_S5 * batch, 2 * S5_LANES), F32), pltpu.VMEM(x0.shape, F32)],
        compiler_params=pltpu.CompilerParams(dimension_semantics=("arbitrary",), vmem_limit_bytes=VMEM_LIMIT),
        name="s5_bwd" if reverse else "s5_fwd",
    )(*args)


def _s5(u_tm, x0, prm):
    y_b, xf_b = _s5_dir(u_tm, x0[1], prm["bblk"][1], prm["a"][1], prm["cblk"], None, True)
    y, xf_f = _s5_dir(u_tm, x0[0], prm["bblk"][0], prm["a"][0], prm["cblk"],
                      (y_b, prm["d"], prm["wglu"], prm["bglu"]), False)
    return y, jnp.stack([xf_f, xf_b])


def _s5_params(a_re, a_im, log_dt, b_re, b_im, c_re, c_im, d, w_glu, b_glu):
    dt = jnp.exp(log_dt)[..., None]
    mag = jnp.exp(dt * a_re)
    ab_re = mag * jnp.cos(dt * a_im)
    ab_im = mag * jnp.sin(dt * a_im)
    den = a_re * a_re + a_im * a_im
    num_re = ab_re - 1.0
    num_im = ab_im
    f_re = (num_re * a_re + num_im * a_im) / den
    f_im = (num_im * a_re - num_re * a_im) / den
    bb_re = f_re[..., None] * b_re - f_im[..., None] * b_im
    bb_im = f_re[..., None] * b_im + f_im[..., None] * b_re
    eye = jnp.eye(S5_GROUPS, dtype=F32)

    def in_block(bb):
        return jnp.einsum("dgph,gk->dghkp", bb, eye).reshape(2, S5_WIDTH, S5_LANES)

    def out_block(c):
        return jnp.einsum("ghp,gk->gpkh", c, eye).reshape(S5_LANES, S5_WIDTH)

    def stream_major(re, im, axis):
        parts = []
        for h in range(S5_STREAMS):
            idx = [slice(None)] * re.ndim
            idx[axis] = slice(h * S5_HALF, (h + 1) * S5_HALF)
            parts += [re[tuple(idx)], im[tuple(idx)]]
        return jnp.concatenate(parts, axis=axis)

    return {
        "a": jnp.stack([ab_re.reshape(2, S5_LANES), ab_im.reshape(2, S5_LANES)], axis=1),
        "bblk": stream_major(in_block(bb_re), in_block(bb_im), 2).astype(BF16),
        "cblk": stream_major(out_block(c_re), -out_block(c_im), 0).astype(BF16),
        "d": d.reshape(1, S5_WIDTH),
        "wglu": w_glu.astype(BF16),
        "bglu": b_glu.reshape(1, S5_WIDTH),
    }


def _pad_heads(w, heads, width, padded):
    lead = w.shape[:-1]
    w = w.reshape(lead + (heads, width))
    w = jnp.pad(w, [(0, 0)] * len(lead) + [(0, 0), (0, padded - width)])
    return w.reshape(lead + (heads * padded,))


def _pad_last(w, width):
    return jnp.pad(w, [(0, 0)] * (w.ndim - 1) + [(0, width - w.shape[-1])])


def _pack_w_in(w_in):
    pts = []
    acc = 0
    for s in IN_SPLITS:
        pts.append((acc, acc + s))
        acc += s
    q, k, v, r, glr, s5, z, xbc, dt = (w_in[:, a:b] for a, b in pts)
    return jnp.concatenate([
        _pad_heads(q, GLA_HEADS, GLA_DK, GLA_DKP), _pad_heads(k, GLA_HEADS, GLA_DK, GLA_DKP), v, r,
        _pad_last(glr, P_GLR), s5, z, xbc, _pad_last(dt, P_DT)], axis=-1).astype(BF16)


def _pack_gla_gate(w_gate, b_gate):
    w = _pad_heads(w_gate, GLA_HEADS, GLA_DK, GLA_DKP)
    wp = jnp.zeros((2, P_GLR, GLA_QKP), F32)
    for d in range(2):
        wp = wp.at[d, d * GLA_RANK:(d + 1) * GLA_RANK].set(w[d])
    return wp.astype(BF16), _pad_heads(b_gate, GLA_HEADS, GLA_DK, GLA_DKP).reshape(2, 1, GLA_QKP)


def _per_seq(t, batch):
    n, ch = t.shape
    return t.reshape(batch, n // batch, ch)


def _to_col_major(t):
    batch, n, ch = t.shape
    return t.reshape(batch, n // GRID_W, GRID_W, ch).transpose(0, 2, 1, 3).reshape(batch, n, ch)


def _from_col_major(t):
    batch, n, ch = t.shape
    return t.reshape(batch, GRID_W, n // GRID_W, ch).transpose(0, 2, 1, 3).reshape(batch, n, ch)


def _to_time_major(t, batch):
    n, ch = t.shape
    return t.reshape(batch, n // batch, ch).transpose(1, 0, 2)


def _from_time_major(t):
    seq, batch, ch = t.shape
    return t.transpose(1, 0, 2).reshape(seq * batch, ch)


def kernel(x, c, ctx, c_ctx, ada_w, ada_b, norm_g, w_in, w_out, ff_w_gate, ff_w_up, ff_w_down, gla_w_gate, gla_b_gate, gla_norm_g, s5_a_re, s5_a_im, s5_log_dt, s5_b_re, s5_b_im, s5_c_re, s5_c_im, s5_d, s5_w_glu, s5_b_glu, ssd_conv_w, ssd_conv_b, ssd_dt_bias, ssd_a_log, ssd_d, ssd_norm_g, final_norm_g):
    batch, seq, _ = x.shape
    ctx_len = ctx.shape[1]
    depth = ada_w.shape[0]
    assert seq == GRID_W * CHUNK and ctx_len % TB_SCAN == 0 and seq % TM_TOKENS == 0
    assert batch == SUBLANE and (batch * ctx_len) % TM_TOKENS == 0 and batch % SCAN_BATCH == 0

    mod_rows = 2 * SUBLANE
    cc = jnp.zeros((mod_rows, D_MODEL), F32).at[:batch].set(c).at[batch].set(c_ctx)
    mods = _ada(cc, ada_w, ada_b).reshape(depth, mod_rows, N_MOD, D_MODEL)
    lat_mod = _mod_spec(seq // TM_TOKENS, None)
    ctx_mod = _mod_spec(None, batch)

    h = x.reshape(batch * seq, D_MODEL)
    hc = ctx.reshape(batch * ctx_len, D_MODEL)
    fg = final_norm_g.reshape(1, D_MODEL)

    for i in range(depth):
        ctx_out = i < depth - 1
        mod = mods[i]
        ng = norm_g[i]
        wg = ff_w_gate[i].astype(BF16)
        wu = ff_w_up[i].astype(BF16)
        wd = ff_w_down[i].astype(BF16)
        w_all = _pack_w_in(w_in[i])
        wo = w_out[i].astype(BF16)
        wog, wos, woc = wo[:GLA_WIDTH], wo[GLA_WIDTH:GLA_WIDTH + S5_WIDTH], wo[GLA_WIDTH + S5_WIDTH:]
        gla_wg, gla_bg = _pack_gla_gate(gla_w_gate[i], gla_b_gate[i])
        gla_ng = gla_norm_g[i].reshape(1, GLA_WIDTH)
        s5p = _s5_params(s5_a_re[i], s5_a_im[i], s5_log_dt[i], s5_b_re[i], s5_b_im[i], s5_c_re[i], s5_c_im[i],
                         s5_d[i], s5_w_glu[i], s5_b_glu[i])
        cw = jnp.pad(ssd_conv_w[i], ((0, SUBLANE - SSD_CONV), (0, 0)))
        cb = ssd_conv_b[i].reshape(1, SSD_XBC)
        dtb = _pad_last(ssd_dt_bias[i].reshape(1, 2 * SSD_HEADS), P_DT)
        alog = _pad_last(ssd_a_log[i].reshape(1, 2 * SSD_HEADS), P_DT)
        dsk = jnp.repeat(ssd_d[i], SSD_HEADDIM).reshape(1, SSD_WIDTH)
        ssd_ng = ssd_norm_g[i].reshape(1, SSD_WIDTH)

        h = _ffn1(h, mod, lat_mod, ng, wg[0], wu[0], wd[0])
        hc = _ffn1(hc, mod, ctx_mod, ng, wg[0], wu[0], wd[0])

        qk_l, v_l, r_l, glr_l, s5_l, z_l, xbc_l, dt_l = _inproj(h, mod, lat_mod, ng, w_all)
        qk_c, v_c, r_c, glr_c, s5_c, z_c, xbc_c, dt_c = _inproj(hc, mod, ctx_mod, ng, w_all)

        gla_s0 = jnp.zeros((batch, 2, GLA_WIDTH, GLA_QKP), F32)
        seqs = functools.partial(_per_seq, batch=batch)
        yg_c, gla_s = _gla(seqs(qk_c), seqs(v_c), seqs(r_c), seqs(glr_c), gla_s0, gla_wg, gla_bg, gla_ng)
        yg_l, _ = _gla(seqs(qk_l), seqs(v_l), seqs(r_l), seqs(glr_l), gla_s, gla_wg, gla_bg, gla_ng)
        yg_c = yg_c.reshape(-1, GLA_WIDTH)
        yg_l = yg_l.reshape(-1, GLA_WIDTH)

        s5_x0 = jnp.zeros((2, batch, 2 * S5_LANES), F32)
        ys_c, s5_x = _s5(_to_time_major(s5_c, batch), s5_x0, s5p)
        ys_l, _ = _s5(_to_time_major(s5_l, batch), s5_x, s5p)
        ys_l = _from_time_major(ys_l)

        ssd_s0 = jnp.zeros((batch, 2, SSD_BC, SSD_WIDTH), F32)
        yc_c, ssd_s = _ssd(seqs(z_c), seqs(xbc_c), seqs(dt_c), ssd_s0, cw, cb, dtb, alog, dsk, ssd_ng)
        yc_l, _ = _ssd(_to_col_major(seqs(z_l)), _to_col_major(seqs(xbc_l)), _to_col_major(seqs(dt_l)),
                       ssd_s, cw, cb, dtb, alog, dsk, ssd_ng)
        yc_c = yc_c.reshape(-1, SSD_WIDTH)
        yc_l = _from_col_major(yc_l).reshape(-1, SSD_WIDTH)

        h = _mix_ffn2(h, yg_l, ys_l, yc_l, mod, lat_mod, ng, wog, wos, woc, wg[1], wu[1], wd[1], fg,
                      final=not ctx_out)
        if ctx_out:
            hc = _mix_ffn2(hc, yg_c, _from_time_major(ys_c), yc_c, mod, ctx_mod, ng, wog, wos, woc,
                           wg[1], wu[1], wd[1], fg, final=False)
    return h.reshape(batch, seq, D_MODEL)
```

```python
import functools
import math

import jax
import jax.numpy as jnp
from jax import lax
from jax.experimental import pallas as pl
from jax.experimental.pallas import tpu as pltpu

F32 = jnp.float32
BF16 = jnp.bfloat16
HI = lax.Precision.HIGHEST

D_MODEL = 1024
GRID_W = 64
N_MOD = 9
D_FF = 2816
EPS = 1e-6

GLA_HEADS = 4
GLA_DK = 48
GLA_DKP = 64
GLA_DV = 96
GLA_RANK = 16
GLA_TAU = 16.0
GLA_QK = GLA_HEADS * GLA_DK
GLA_QKP = GLA_HEADS * GLA_DKP
GLA_WIDTH = GLA_HEADS * GLA_DV

S5_WIDTH = 256
S5_GROUP = 16
S5_GROUPS = 16
S5_STATE = 64
S5_LANES = S5_GROUPS * S5_STATE
S5_STREAMS = 4
S5_HALF = S5_LANES // S5_STREAMS
S5_PARTS = 2

SSD_HEADS = 6
SSD_HEADDIM = 64
SSD_GROUPS = 2
SSD_REP = SSD_HEADS // SSD_GROUPS
SSD_STATE = 128
SSD_CONV = 5
SSD_WIDTH = SSD_HEADS * SSD_HEADDIM
SSD_BC = SSD_GROUPS * SSD_STATE
SSD_XBC = SSD_WIDTH + 2 * SSD_BC

MIX_WIDTH = GLA_WIDTH + S5_WIDTH + SSD_WIDTH
IN_SPLITS = (GLA_QK, GLA_QK, GLA_WIDTH, GLA_WIDTH, 2 * GLA_RANK, S5_WIDTH, SSD_WIDTH, SSD_XBC, 2 * SSD_HEADS)

CHUNK = 64
LANE = 128
SUBLANE = 8
HALO = SUBLANE

P_QK = 2 * GLA_QKP
P_GLR = LANE
P_DT = LANE
P_TOTAL = P_QK + 2 * GLA_WIDTH + P_GLR + S5_WIDTH + SSD_WIDTH + SSD_XBC + P_DT

TM_TOKENS = 512
TB_SCAN = 256
SCAN_BATCH = 2
TT_S5 = 128
VMEM_LIMIT = 56 * 1024 * 1024


def _silu(x):
    return x * jax.nn.sigmoid(x)


def _rms(x, g):
    return x * lax.rsqrt(jnp.mean(x * x, axis=-1, keepdims=True) + EPS) * g


def _dot(a, b):
    return jnp.dot(a.astype(BF16), b.astype(BF16), preferred_element_type=F32)


def _dot_nt(a, b):
    return lax.dot_general(a.astype(BF16), b.astype(BF16), (((1,), (1,)), ((), ())), preferred_element_type=F32)


def _dot_tn(a, b):
    return lax.dot_general(a.astype(BF16), b.astype(BF16), (((0,), (0,)), ((), ())), preferred_element_type=F32)


def _dot_hi(a, b):
    return jnp.dot(a, b, precision=HI, preferred_element_type=F32)


def _split3(x):
    hi = x.astype(BF16)
    rest = x - hi.astype(F32)
    mid = rest.astype(BF16)
    lo = (rest - mid.astype(F32)).astype(BF16)
    return hi, mid, lo


def _dot_exact_rhs(x, m3):
    return jnp.dot(jnp.concatenate(_split3(x), axis=1), m3, preferred_element_type=F32)


def _dot_exact_lhs(m3, x):
    return jnp.dot(m3, jnp.concatenate(_split3(x), axis=0), preferred_element_type=F32)


_DONE = object()


def _interleave(*streams):
    live = list(streams)
    while live:
        for stream in tuple(live):
            if next(stream, _DONE) is _DONE:
                live.remove(stream)


def _lagged(stream, stages):
    for _ in range(stages):
        yield
    yield from stream


def _iota(shape, dim):
    return lax.broadcasted_iota(jnp.int32, shape, dim)


def _block_id(shape, dim, width):
    idx = _iota(shape, dim)
    out = jnp.zeros(shape, F32)
    for k in range(1, -(-shape[dim] // width)):
        out = out + jnp.where(idx >= k * width, 1.0, 0.0)
    return out


def _chunk_tri(n, lower):
    i = _iota((n, n), 0)
    j = _iota((n, n), 1)
    same_chunk = (i & -CHUNK) == (j & -CHUNK)
    return jnp.where(same_chunk & ((j <= i) if lower else (j >= i)), 1.0, 0.0).astype(BF16)


def _ada_kernel(c_ref, w_ref, b_ref, o_ref):
    o_ref[0] = _dot_hi(_silu(c_ref[...]), w_ref[0]) + b_ref[0]


def _ada(cc, ada_w, ada_b):
    depth = ada_w.shape[0]
    rows = cc.shape[0]
    tn = D_MODEL
    return pl.pallas_call(
        _ada_kernel,
        grid=(depth, N_MOD * D_MODEL // tn),
        in_specs=[pl.BlockSpec((rows, D_MODEL), lambda l, n: (0, 0)),
                  pl.BlockSpec((1, D_MODEL, tn), lambda l, n: (l, 0, n)),
                  pl.BlockSpec((1, 1, tn), lambda l, n: (l, 0, n))],
        out_specs=pl.BlockSpec((1, rows, tn), lambda l, n: (l, 0, n)),
        out_shape=jax.ShapeDtypeStruct((depth, rows, N_MOD * D_MODEL), F32),
        name="ada_mod",
    )(cc, ada_w, ada_b.reshape(depth, 1, N_MOD * D_MODEL))


def _ffn_body(x, m, g, wg_ref, wu_ref, wd_ref, base):
    u = (_rms(x, g) * (1.0 + m[base + 1:base + 2]) + m[base:base + 1]).astype(BF16)
    gate = jnp.dot(u, wg_ref[...], preferred_element_type=F32)
    up = jnp.dot(u, wu_ref[...], preferred_element_type=F32)
    act = (_silu(gate) * up).astype(BF16)
    y = jnp.dot(act, wd_ref[...], preferred_element_type=F32)
    return x + 0.5 * m[base + 2:base + 3] * y


def _ffn1_kernel(h_ref, mod_ref, ng_ref, wg_ref, wu_ref, wd_ref, o_ref):
    o_ref[...] = _ffn_body(h_ref[...], mod_ref[0], ng_ref[0:1], wg_ref, wu_ref, wd_ref, 0)


def _mix_ffn2_kernel(h_ref, yg_ref, ys_ref, yc_ref, mod_ref, ng_ref, wo_ref,
                     wg_ref, wu_ref, wd_ref, fg_ref, o_ref, *, final):
    m = mod_ref[0]
    mixed = jnp.concatenate([yg_ref[...], ys_ref[...], yc_ref[...]], axis=1).astype(BF16)
    mix = jnp.dot(mixed, wo_ref[...], preferred_element_type=F32)
    x = h_ref[...] + m[5:6] * mix
    x = _ffn_body(x, m, ng_ref[2:3], wg_ref, wu_ref, wd_ref, 6)
    if final:
        x = _rms(x, fg_ref[...])
    o_ref[...] = x


def _const_spec(shape):
    nd = len(shape)
    return pl.BlockSpec(shape, lambda t: (0,) * nd, pipeline_mode=pl.Buffered(1))


def _layer_spec(stacked, *lead):
    tail = stacked.shape[len(lead):]
    return pl.BlockSpec((None,) * len(lead) + tail, lambda t: lead + (0,) * len(tail),
                        pipeline_mode=pl.Buffered(1))


def _tile_spec(width):
    return pl.BlockSpec((TM_TOKENS, width), lambda t: (t, 0))


def _mod_spec(tiles_per_row, fixed_row):
    if fixed_row is None:
        return pl.BlockSpec((1, N_MOD, D_MODEL), lambda t: (t // tiles_per_row, 0, 0))
    return pl.BlockSpec((1, N_MOD, D_MODEL), lambda t: (fixed_row, 0, 0))


def _dense_params():
    return pltpu.CompilerParams(dimension_semantics=("arbitrary",), vmem_limit_bytes=VMEM_LIMIT)


def _ffn1(h, mod, mod_spec, ng, wg, wu, wd, layer):
    n = h.shape[0]
    return pl.pallas_call(
        _ffn1_kernel,
        grid=(n // TM_TOKENS,),
        in_specs=[_tile_spec(D_MODEL), mod_spec, _const_spec(ng.shape),
                  _layer_spec(wg, layer, 0), _layer_spec(wu, layer, 0), _layer_spec(wd, layer, 0)],
        out_specs=_tile_spec(D_MODEL),
        out_shape=jax.ShapeDtypeStruct(h.shape, F32),
        compiler_params=_dense_params(),
        name="ffn1",
    )(h, mod, ng, wg, wu, wd)


def _mix_ffn2(h, yg, ys, yc, mod, mod_spec, ng, wo, wg, wu, wd, fg, layer, final):
    n = h.shape[0]
    return pl.pallas_call(
        functools.partial(_mix_ffn2_kernel, final=final),
        grid=(n // TM_TOKENS,),
        in_specs=[_tile_spec(D_MODEL), _tile_spec(GLA_WIDTH), _tile_spec(S5_WIDTH), _tile_spec(SSD_WIDTH),
                  mod_spec, _const_spec(ng.shape), _layer_spec(wo, layer),
                  _layer_spec(wg, layer, 1), _layer_spec(wu, layer, 1), _layer_spec(wd, layer, 1),
                  _const_spec(fg.shape)],
        out_specs=_tile_spec(D_MODEL),
        out_shape=jax.ShapeDtypeStruct(h.shape, F32),
        compiler_params=_dense_params(),
        name="mix_ffn2",
    )(h, yg, ys, yc, mod, ng, wo, wg, wu, wd, fg)


_P_WIDTHS = (P_QK, GLA_WIDTH, GLA_WIDTH, P_GLR, S5_WIDTH, SSD_WIDTH, SSD_XBC, P_DT)


def _inproj_kernel(h_ref, mod_ref, ng_ref, w_ref, *out_refs):
    m = mod_ref[0]
    u = _rms(h_ref[...], ng_ref[1:2]) * (1.0 + m[4:5]) + m[3:4]
    p = jnp.dot(u.astype(BF16), w_ref[...], preferred_element_type=F32)
    off = 0
    for ref, width in zip(out_refs, _P_WIDTHS):
        ref[...] = p[:, off:off + width]
        off += width


def _inproj(h, mod, mod_spec, ng, w_all, layer):
    n = h.shape[0]
    return pl.pallas_call(
        _inproj_kernel,
        grid=(n // TM_TOKENS,),
        in_specs=[_tile_spec(D_MODEL), mod_spec, _const_spec(ng.shape), _layer_spec(w_all, layer)],
        out_specs=[_tile_spec(w) for w in _P_WIDTHS],
        out_shape=[jax.ShapeDtypeStruct((n, w), F32) for w in _P_WIDTHS],
        compiler_params=_dense_params(),
        name="inproj",
    )(h, mod, ng, w_all)


def _scan_blk(ph, j, nblk):
    return jnp.where(ph == 0, nblk - 1 - j, j)


def _seq_spec(width, nblk):
    return pl.BlockSpec((SCAN_BATCH, TB_SCAN, width), lambda g, ph, j: (g, _scan_blk(ph, j, nblk), 0))


def _phase0_blk(ph, j, nblk):
    return jnp.where(ph == 0, nblk - 1 - j, 0)


def _seq_phase0_spec(width, nblk):
    return pl.BlockSpec((SCAN_BATCH, TB_SCAN, width), lambda g, ph, j: (g, _phase0_blk(ph, j, nblk), 0))


def _seq_phase1_spec(width, nblk):
    return pl.BlockSpec((SCAN_BATCH, TB_SCAN, width), lambda g, ph, j: (g, jnp.where(ph == 0, 0, j), 0))


def _state_spec(shape):
    return pl.BlockSpec((SCAN_BATCH,) + shape, lambda g, ph, j: (g,) + (0,) * len(shape))


def _param_spec(shape):
    nd = len(shape)
    return pl.BlockSpec(shape, lambda b, ph, j: (0,) * nd, pipeline_mode=pl.Buffered(1))


def _scan_consts():
    tri3 = jnp.stack([jnp.concatenate([_chunk_tri(TB_SCAN, lower=(d == 0))] * 3, axis=1) for d in range(2)])
    same_head = (_block_id((GLA_WIDTH, GLA_WIDTH), 0, GLA_DV)
                 == _block_id((GLA_WIDTH, GLA_WIDTH), 1, GLA_DV)).astype(BF16)
    gla_block = (_block_id((GLA_WIDTH, GLA_QKP), 0, GLA_DV)
                 == _block_id((GLA_WIDTH, GLA_QKP), 1, GLA_DKP)).astype(F32)
    expand = [jnp.where(_iota((LANE, SSD_WIDTH), 0) == d * SSD_HEADS + _block_id((LANE, SSD_WIDTH), 1, SSD_HEADDIM),
                        1.0, 0.0).astype(BF16) for d in range(2)]
    return {
        "tri3": tri3,
        "same_head3": jnp.concatenate([same_head] * 3, axis=0),
        "gla_block": gla_block,
        "expand3": jnp.stack([jnp.concatenate([e] * 3, axis=0) for e in expand]),
    }


def _scan_params():
    return pltpu.CompilerParams(dimension_semantics=("arbitrary", "arbitrary", "arbitrary"),
                                vmem_limit_bytes=VMEM_LIMIT)


def _gla_kernel(qk_ref, v_ref, r_ref, glr_ref, s0_ref, wg_ref, bg_ref, ng_ref, tri_ref, sameh_ref, bmask_ref,
                y_ref, sfin_ref, st_ref, ob_ref, qkb_ref, *, nblk):
    ph = pl.program_id(1)
    j = pl.program_id(2)
    nch = TB_SCAN // CHUNK
    scale = GLA_DK ** -0.5

    lane_k = _block_id((1, GLA_QKP), 1, GLA_DKP)
    lane_v = _block_id((1, GLA_WIDTH), 1, GLA_DV)
    kmask16 = [(lane_k == h).astype(BF16) for h in range(GLA_HEADS)]
    vmask16 = [(lane_v == h).astype(BF16) for h in range(GLA_HEADS)]
    blockmask = bmask_ref[...]

    def decay_cum(s, d):
        g = _dot(glr_ref[s], wg_ref[d]) + bg_ref[d]
        return _dot_exact_lhs(tri_ref[d], jax.nn.log_sigmoid(g) * (1.0 / GLA_TAU))

    def rows(x, c):
        return x[c * CHUNK:(c + 1) * CHUNK]

    def per_head_rows(x, masks):
        return jnp.concatenate([x * m for m in masks], axis=0)

    def backward_sweep(s):
        base = pl.multiple_of((nblk - 1 - j) * TB_SCAN, TB_SCAN)
        cum = decay_cum(s, 1)
        yield
        q_dec = (qk_ref[s, :, :GLA_QKP] * scale * jnp.exp(cum)).astype(BF16)
        k_inv = qk_ref[s, :, GLA_QKP:] * jnp.exp(-cum)
        v = v_ref[s]
        qkb_ref[s, pl.ds(base, TB_SCAN), :GLA_QKP] = q_dec
        qkb_ref[s, pl.ds(base, TB_SCAN), GLA_QKP:] = k_inv.astype(BF16)
        state = st_ref[s]
        yield
        for c in reversed(range(nch)):
            ob_ref[s, pl.ds(base + c * CHUNK, CHUNK), :] = _dot_nt(rows(q_dec, c), state)
            yield
            decay = jnp.exp(rows(cum, c)[0:1, :])
            state = state * decay + _dot_tn(rows(v, c), rows(k_inv, c) * decay) * blockmask
            yield
        st_ref[s] = state

    def forward_sweep(s):
        base = pl.multiple_of(j * TB_SCAN, TB_SCAN)
        row_i = _iota((CHUNK, GLA_HEADS * CHUNK), 0)
        lane_j = _iota((CHUNK, GLA_HEADS * CHUNK), 1) & (CHUNK - 1)
        cum_f = decay_cum(s, 0)
        yield
        q_f = qk_ref[s, :, :GLA_QKP] * scale * jnp.exp(cum_f)
        k_f = qk_ref[s, :, GLA_QKP:] * jnp.exp(-cum_f)
        k_f16 = k_f.astype(BF16)
        v = v_ref[s]
        v16 = v.astype(BF16)
        q_b = qkb_ref[s, pl.ds(base, TB_SCAN), :GLA_QKP]
        k_b = qkb_ref[s, pl.ds(base, TB_SCAN), GLA_QKP:]
        state = st_ref[s]
        outs = []
        yield
        for c in range(nch):
            a_f = _dot_nt(rows(q_f, c), per_head_rows(rows(k_f16, c), kmask16))
            a_b = _dot_nt(rows(q_b, c), per_head_rows(rows(k_b, c), kmask16))
            yield
            scores = jnp.where(lane_j <= row_i, a_f, 0.0) + jnp.where(lane_j >= row_i, a_b, 0.0)
            o = _dot(scores, per_head_rows(rows(v16, c), vmask16)) + _dot_nt(rows(q_f, c), state)
            outs.append(o)
            yield
            decay = jnp.exp(rows(cum_f, c)[CHUNK - 1:CHUNK, :])
            state = state * decay + _dot_tn(rows(v, c), rows(k_f, c) * decay) * blockmask
            yield
        st_ref[s] = state
        o = jnp.concatenate(outs, axis=0) + ob_ref[s, pl.ds(base, TB_SCAN), :]
        mean_sq = _dot_exact_rhs(o * o, sameh_ref[...]) * (1.0 / GLA_DV)
        yield
        y_ref[s] = o * lax.rsqrt(mean_sq + EPS) * ng_ref[...] * _silu(r_ref[s])

    @pl.when(j == 0)
    def _():
        st_ref[...] = jnp.where(ph == 0, s0_ref[:, 1], s0_ref[:, 0])

    @pl.when(ph == 0)
    def _():
        _interleave(*(backward_sweep(s) for s in range(SCAN_BATCH)))

    @pl.when(ph == 1)
    def _():
        _interleave(*(forward_sweep(s) for s in range(SCAN_BATCH)))

    @pl.when(j == nblk - 1)
    def _():
        @pl.when(ph == 0)
        def _():
            sfin_ref[:, 1] = st_ref[...]

        @pl.when(ph == 1)
        def _():
            sfin_ref[:, 0] = st_ref[...]


def _gla(qk, v, r, glr, s0, wg, bg, ng, consts):
    batch, seq, _ = qk.shape
    nblk = seq // TB_SCAN
    sshape = (2, GLA_WIDTH, GLA_QKP)
    tri3, same_head3, gla_block = consts["tri3"], consts["same_head3"], consts["gla_block"]
    return pl.pallas_call(
        functools.partial(_gla_kernel, nblk=nblk),
        grid=(batch // SCAN_BATCH, 2, nblk),
        in_specs=[_seq_spec(P_QK, nblk), _seq_spec(GLA_WIDTH, nblk), _seq_phase1_spec(GLA_WIDTH, nblk),
                  _seq_spec(P_GLR, nblk), _state_spec(sshape),
                  _param_spec(wg.shape), _param_spec(bg.shape), _param_spec(ng.shape),
                  _param_spec(tri3.shape), _param_spec(same_head3.shape), _param_spec(gla_block.shape)],
        out_specs=[_seq_phase1_spec(GLA_WIDTH, nblk), _state_spec(sshape)],
        out_shape=[jax.ShapeDtypeStruct((batch, seq, GLA_WIDTH), F32),
                   jax.ShapeDtypeStruct((batch,) + sshape, F32)],
        scratch_shapes=[pltpu.VMEM((SCAN_BATCH, GLA_WIDTH, GLA_QKP), F32),
                        pltpu.VMEM((SCAN_BATCH, seq, GLA_WIDTH), F32),
                        pltpu.VMEM((SCAN_BATCH, seq, P_QK), BF16)],
        compiler_params=_scan_params(),
        name="gla",
    )(qk, v, r, glr, s0, wg, bg, ng, tri3, same_head3, gla_block)


def _ssd_kernel(z_ref, x_ref, xp_ref, xn_ref, dt_ref, s0_ref, cw_ref, cb_ref, dtb_ref, alog_ref, dsk_ref, ng_ref,
                tri_ref, exp_ref, y_ref, sfin_ref, st_ref, yb_ref, ext_ref, xs_ref, bc_ref, *, nblk):
    ph = pl.program_id(1)
    j = pl.program_id(2)
    blk = _scan_blk(ph, j, nblk)
    base = pl.multiple_of(blk * TB_SCAN, TB_SCAN)
    nch = TB_SCAN // CHUNK

    lane_h = _block_id((1, SSD_WIDTH), 1, SSD_HEADDIM)
    lane_g = _block_id((1, SSD_WIDTH), 1, SSD_HEADDIM * SSD_REP)
    lane_n = _block_id((1, SSD_BC), 1, SSD_STATE)
    head_lanes = [(lane_h == h).astype(BF16) for h in range(SSD_HEADS)]
    group_lanes = [(lane_g == g).astype(BF16) for g in range(SSD_GROUPS)]
    group_states = [(lane_n == g).astype(BF16) for g in range(SSD_GROUPS)]

    def rows(x, c):
        return x[c * CHUNK:(c + 1) * CHUNK]

    def conv_silu(s):
        ext_ref[s, 0:HALO, :] = jnp.where(blk > 0, xp_ref[s], 0.0)
        ext_ref[s, HALO:HALO + TB_SCAN, :] = x_ref[s]
        ext_ref[s, HALO + TB_SCAN:, :] = jnp.where(blk < nblk - 1, xn_ref[s], 0.0)
        pad = (SSD_CONV - 1) // 2
        acc = cb_ref[...]
        for t in range(SSD_CONV):
            acc = acc + ext_ref[s, pl.ds(HALO - pad + t, TB_SCAN), :] * cw_ref[t:t + 1, :]
        return _silu(acc)

    def decay_inputs(s):
        dt_c = jax.nn.softplus(dt_ref[s] + dtb_ref[...])
        return dt_c, dt_c * -jnp.exp(alog_ref[...])

    def expanded(d, cum_c, dt_c):
        both = _dot_exact_rhs(jnp.concatenate([cum_c, dt_c], axis=0), exp_ref[d])
        return both[:TB_SCAN], both[TB_SCAN:]

    def scan_terms(d, cum_e, dt_e):
        cum = cum_e.reshape(nch, CHUNK, SSD_WIDTH)
        last = cum[:, CHUNK - 1:CHUNK, :] if d == 0 else cum[:, 0:1, :]
        weight = jnp.exp(last - cum).reshape(TB_SCAN, SSD_WIDTH) * dt_e
        return jnp.exp(cum_e), weight, jnp.exp(last)

    def advance(state, decay, bm_c, xw_c):
        xw16 = xw_c.astype(BF16)
        contrib = [_dot_tn(bm_c[:, g * SSD_STATE:(g + 1) * SSD_STATE], xw16 * group_lanes[g])
                   for g in range(SSD_GROUPS)]
        return state * decay + jnp.concatenate(contrib, axis=0)

    def backward_sweep(s):
        xbc = conv_silu(s)
        yield
        dt_c, la_c = decay_inputs(s)
        cum_c = _dot_exact_lhs(tri_ref[1], la_c)
        yield
        xs = xbc[:, :SSD_WIDTH]
        bc16 = xbc[:, SSD_WIDTH:].astype(BF16)
        xs_ref[s, pl.ds(base, TB_SCAN), :] = xs
        bc_ref[s, pl.ds(base, TB_SCAN), :] = bc16
        bm, cm = bc16[:, :SSD_BC], bc16[:, SSD_BC:]
        cum_e, dt_e = expanded(1, cum_c, dt_c)
        yield
        grow, weight, decay = scan_terms(1, cum_e, dt_e)
        xw = xs * weight
        state = st_ref[s]
        outs = [None] * nch
        yield
        for c in reversed(range(nch)):
            outs[c] = _dot(rows(cm, c), state)
            state = advance(state, decay[c], rows(bm, c), rows(xw, c))
            yield
        st_ref[s] = state
        yb_ref[s, pl.ds(base, TB_SCAN), :] = jnp.concatenate(outs, axis=0) * grow

    def forward_sweep(s):
        xs = xs_ref[s, pl.ds(base, TB_SCAN), :]
        bm = bc_ref[s, pl.ds(base, TB_SCAN), :SSD_BC]
        cm = bc_ref[s, pl.ds(base, TB_SCAN), SSD_BC:]
        dt_c, la_c = decay_inputs(s)
        pieces = jnp.concatenate(_split3(la_c), axis=0)
        cum_cf = jnp.dot(tri_ref[0], pieces, preferred_element_type=F32)
        cum_cb = jnp.dot(tri_ref[1], pieces, preferred_element_type=F32)
        yield
        cum_f, dt_f = expanded(0, cum_cf, dt_c)
        yield
        cum_b, dt_b = expanded(1, cum_cb, dt_c)
        yield
        grow, weight, decay = scan_terms(0, cum_f, dt_f)
        xw = xs * weight
        ci = _iota((CHUNK, SSD_WIDTH), 0)
        cj = _iota((CHUNK, SSD_WIDTH), 1) & (CHUNK - 1)
        diag = (_iota((TB_SCAN, SSD_WIDTH), 1) & (CHUNK - 1)) == (_iota((TB_SCAN, SSD_WIDTH), 0) & (CHUNK - 1))

        def at_j(x):
            return jnp.sum(jnp.where(diag, x, 0.0).reshape(nch, CHUNK, SSD_WIDTH), axis=1, keepdims=True)

        cum_row_f, cum_row_b, dt_row_f, dt_row_b = at_j(cum_f), at_j(cum_b), at_j(dt_f), at_j(dt_b)
        xs16 = xs.astype(BF16)
        state = st_ref[s]
        outs = []
        yield
        for c in range(nch):
            bm_group = [rows(bm, c) * group_states[g] for g in range(SSD_GROUPS)]
            bm_heads = jnp.concatenate([bm_group[h // SSD_REP] for h in range(SSD_HEADS)], axis=0)
            scores = _dot_nt(rows(cm, c), bm_heads)
            yield
            l_f = jnp.exp(jnp.where(cj <= ci, rows(cum_f, c) - cum_row_f[c], -jnp.inf))
            l_b = jnp.exp(jnp.where(cj >= ci, rows(cum_b, c) - cum_row_b[c], -jnp.inf))
            mat = scores * (l_f * dt_row_f[c] + l_b * dt_row_b[c])
            x_heads = jnp.concatenate([rows(xs16, c) * head_lanes[h] for h in range(SSD_HEADS)], axis=0)
            outs.append(_dot(mat, x_heads) + _dot(rows(cm, c), state) * rows(grow, c))
            yield
            state = advance(state, decay[c], rows(bm, c), rows(xw, c))
            yield
        st_ref[s] = state
        y = jnp.concatenate(outs, axis=0) + yb_ref[s, pl.ds(base, TB_SCAN), :] + xs * dsk_ref[...]
        y_ref[s] = _rms(y * _silu(z_ref[s]), ng_ref[...])

    @pl.when(j == 0)
    def _():
        st_ref[...] = jnp.where(ph == 0, s0_ref[:, 1], s0_ref[:, 0])

    @pl.when(ph == 0)
    def _():
        _interleave(*(backward_sweep(s) for s in range(SCAN_BATCH)))

    @pl.when(ph == 1)
    def _():
        _interleave(*(forward_sweep(s) for s in range(SCAN_BATCH)))

    @pl.when(j == nblk - 1)
    def _():
        @pl.when(ph == 0)
        def _():
            sfin_ref[:, 1] = st_ref[...]

        @pl.when(ph == 1)
        def _():
            sfin_ref[:, 0] = st_ref[...]


def _ssd(z, xbc, dt, s0, cw, cb, dtb, alog, dsk, ng, consts):
    batch, seq, _ = z.shape
    nblk = seq // TB_SCAN
    tiles = TB_SCAN // HALO
    sshape = (2, SSD_BC, SSD_WIDTH)

    def prev_map(g, ph, j):
        return (g, jnp.maximum(_phase0_blk(ph, j, nblk) * tiles - 1, 0), 0)

    def next_map(g, ph, j):
        return (g, jnp.minimum((_phase0_blk(ph, j, nblk) + 1) * tiles, seq // HALO - 1), 0)

    halo = (SCAN_BATCH, HALO, SSD_XBC)
    tri3, expand3 = consts["tri3"], consts["expand3"]
    return pl.pallas_call(
        functools.partial(_ssd_kernel, nblk=nblk),
        grid=(batch // SCAN_BATCH, 2, nblk),
        in_specs=[_seq_phase1_spec(SSD_WIDTH, nblk), _seq_phase0_spec(SSD_XBC, nblk),
                  pl.BlockSpec(halo, prev_map), pl.BlockSpec(halo, next_map),
                  _seq_spec(P_DT, nblk), _state_spec(sshape),
                  _param_spec(cw.shape), _param_spec(cb.shape), _param_spec(dtb.shape), _param_spec(alog.shape),
                  _param_spec(dsk.shape), _param_spec(ng.shape), _param_spec(tri3.shape), _param_spec(expand3.shape)],
        out_specs=[_seq_phase1_spec(SSD_WIDTH, nblk), _state_spec(sshape)],
        out_shape=[jax.ShapeDtypeStruct((batch, seq, SSD_WIDTH), F32),
                   jax.ShapeDtypeStruct((batch,) + sshape, F32)],
        scratch_shapes=[pltpu.VMEM((SCAN_BATCH, SSD_BC, SSD_WIDTH), F32),
                        pltpu.VMEM((SCAN_BATCH, seq, SSD_WIDTH), F32),
                        pltpu.VMEM((SCAN_BATCH, TB_SCAN + 2 * HALO, SSD_XBC), F32),
                        pltpu.VMEM((SCAN_BATCH, seq, SSD_WIDTH), F32),
                        pltpu.VMEM((SCAN_BATCH, seq, 2 * SSD_BC), BF16)],
        compiler_params=_scan_params(),
        name="ssd",
    )(z, xbc, xbc, xbc, dt, s0, cw, cb, dtb, alog, dsk, ng, tri3, expand3)


def _s5_kernel(*refs, reverse, final, batch):
    if final:
        (u_ref, x0_ref, bblk_ref, a_ref, cblk_ref, yprev_ref, d_ref, wglu_ref, bglu_ref,
         y_ref, xfin_ref, xs_ref, st_ref) = refs
    else:
        u_ref, x0_ref, bblk_ref, a_ref, cblk_ref, y_ref, xfin_ref, xs_ref, st_ref = refs
    i = pl.program_id(0)
    rows = TT_S5 * batch

    @pl.when(i == 0)
    def _():
        st_ref[...] = x0_ref[...]

    u = u_ref[...].reshape(rows, S5_WIDTH)
    u16 = u.astype(BF16)
    piece = 2 * S5_HALF // S5_PARTS
    readout = [None] * S5_STREAMS

    def stream(h):
        lo = h * 2 * S5_HALF
        re = slice(lo, lo + S5_HALF)
        im = slice(lo + S5_HALF, lo + 2 * S5_HALF)
        for p in range(S5_PARTS):
            cols = slice(lo + p * piece, lo + (p + 1) * piece)
            xs_ref[:, cols] = jnp.dot(u16, bblk_ref[:, cols], preferred_element_type=F32)
            yield
        a_re = jnp.broadcast_to(a_ref[0:1, h * S5_HALF:(h + 1) * S5_HALF], (batch, S5_HALF))
        a_im = jnp.broadcast_to(a_ref[1:2, h * S5_HALF:(h + 1) * S5_HALF], (batch, S5_HALF))
        x_re = st_ref[:, re]
        x_im = st_ref[:, im]
        for n in range(TT_S5):
            t = (TT_S5 - 1 - n) if reverse else n
            row = slice(t * batch, (t + 1) * batch)
            x_re, x_im = (a_re * x_re - a_im * x_im + xs_ref[row, re],
                          a_re * x_im + a_im * x_re + xs_ref[row, im])
            xs_ref[row, re] = x_re
            xs_ref[row, im] = x_im
            if (n + 1) % (TT_S5 // S5_PARTS) == 0:
                yield
        st_ref[:, re] = x_re
        st_ref[:, im] = x_im
        xfin_ref[:, re] = x_re
        xfin_ref[:, im] = x_im
        acc = None
        for p in range(S5_PARTS):
            cols = slice(lo + p * piece, lo + (p + 1) * piece)
            part = jnp.dot(xs_ref[:, cols].astype(BF16), cblk_ref[cols, :], preferred_element_type=F32)
            acc = part if acc is None else acc + part
            yield
        readout[h] = acc

    _interleave(*(_lagged(stream(h), h * S5_PARTS) for h in range(S5_STREAMS)))
    y = sum(readout[1:], readout[0])
    if final:
        y = y + yprev_ref[...].reshape(rows, S5_WIDTH) + u * d_ref[...]
        g = jax.nn.gelu(y)
        y = g * jax.nn.sigmoid(_dot(g, wglu_ref[...]) + bglu_ref[...])
    y_ref[...] = y.reshape(TT_S5, batch, S5_WIDTH)


def _s5_dir(u, x0, bblk, a, cblk, extra, reverse):
    seq, batch, _ = u.shape
    nblk = seq // TT_S5
    final = extra is not None

    def seq_map(i):
        return ((nblk - 1 - i) if reverse else i, 0, 0)

    def const(shape):
        nd = len(shape)
        return pl.BlockSpec(shape, lambda i: (0,) * nd)

    seq_spec = pl.BlockSpec((TT_S5, batch, S5_WIDTH), seq_map)
    in_specs = [seq_spec, const(x0.shape), const(bblk.shape), const(a.shape), const(cblk.shape)]
    args = [u, x0, bblk, a, cblk]
    if final:
        yprev, d, wglu, bglu = extra
        in_specs += [seq_spec, const(d.shape), const(wglu.shape), const(bglu.shape)]
        args += [yprev, d, wglu, bglu]
    return pl.pallas_call(
        functools.partial(_s5_kernel, reverse=reverse, final=final, batch=batch),
        grid=(nblk,),
        in_specs=in_specs,
        out_specs=[seq_spec, const(x0.shape)],
        out_shape=[jax.ShapeDtypeStruct(u.shape, F32), jax.ShapeDtypeStruct(x0.shape, F32)],
        scratch_shapes=[pltpu.VMEM((TT_S5 * batch, 2 * S5_LANES), F32), pltpu.VMEM(x0.shape, F32)],
        compiler_params=pltpu.CompilerParams(dimension_semantics=("arbitrary",), vmem_limit_bytes=VMEM_LIMIT),
        name="s5_bwd" if reverse else "s5_fwd",
    )(*args)


def _s5(u_tm, x0, prm):
    y_b, xf_b = _s5_dir(u_tm, x0[1], prm["bblk"][1], prm["a"][1], prm["cblk"], None, True)
    y, xf_f = _s5_dir(u_tm, x0[0], prm["bblk"][0], prm["a"][0], prm["cblk"],
                      (y_b, prm["d"], prm["wglu"], prm["bglu"]), False)
    return y, jnp.stack([xf_f, xf_b])


def _s5_params(a_re, a_im, log_dt, b_re, b_im, c_re, c_im, d, w_glu, b_glu):
    dt = jnp.exp(log_dt)[..., None]
    mag = jnp.exp(dt * a_re)
    ab_re = mag * jnp.cos(dt * a_im)
    ab_im = mag * jnp.sin(dt * a_im)
    den = a_re * a_re + a_im * a_im
    num_re = ab_re - 1.0
    num_im = ab_im
    f_re = (num_re * a_re + num_im * a_im) / den
    f_im = (num_im * a_re - num_re * a_im) / den
    bb_re = f_re[..., None] * b_re - f_im[..., None] * b_im
    bb_im = f_re[..., None] * b_im + f_im[..., None] * b_re
    eye = jnp.eye(S5_GROUPS, dtype=F32)

    def in_block(bb):
        return jnp.einsum("dgph,gk->dghkp", bb, eye).reshape(2, S5_WIDTH, S5_LANES)

    def out_block(c):
        return jnp.einsum("ghp,gk->gpkh", c, eye).reshape(S5_LANES, S5_WIDTH)

    def stream_major(re, im, axis):
        parts = []
        for h in range(S5_STREAMS):
            idx = [slice(None)] * re.ndim
            idx[axis] = slice(h * S5_HALF, (h + 1) * S5_HALF)
            parts += [re[tuple(idx)], im[tuple(idx)]]
        return jnp.concatenate(parts, axis=axis)

    return {
        "a": jnp.stack([ab_re.reshape(2, S5_LANES), ab_im.reshape(2, S5_LANES)], axis=1),
        "bblk": stream_major(in_block(bb_re), in_block(bb_im), 2).astype(BF16),
        "cblk": stream_major(out_block(c_re), -out_block(c_im), 0).astype(BF16),
        "d": d.reshape(1, S5_WIDTH),
        "wglu": w_glu.astype(BF16),
        "bglu": b_glu.reshape(1, S5_WIDTH),
    }


def _pad_heads(w, heads, width, padded):
    lead = w.shape[:-1]
    w = w.reshape(lead + (heads, width))
    w = jnp.pad(w, [(0, 0)] * len(lead) + [(0, 0), (0, padded - width)])
    return w.reshape(lead + (heads * padded,))


def _pad_last(w, width):
    return jnp.pad(w, [(0, 0)] * (w.ndim - 1) + [(0, width - w.shape[-1])])


def _pack_w_in(w_in):
    pts = []
    acc = 0
    for s in IN_SPLITS:
        pts.append((acc, acc + s))
        acc += s
    q, k, v, r, glr, s5, z, xbc, dt = (w_in[..., a:b] for a, b in pts)
    return jnp.concatenate([
        _pad_heads(q, GLA_HEADS, GLA_DK, GLA_DKP), _pad_heads(k, GLA_HEADS, GLA_DK, GLA_DKP), v, r,
        _pad_last(glr, P_GLR), s5, z, xbc, _pad_last(dt, P_DT)], axis=-1).astype(BF16)


def _pack_gla_gate(w_gate, b_gate):
    w = _pad_heads(w_gate, GLA_HEADS, GLA_DK, GLA_DKP)
    wp = jnp.zeros((2, P_GLR, GLA_QKP), F32)
    for d in range(2):
        wp = wp.at[d, d * GLA_RANK:(d + 1) * GLA_RANK].set(w[d])
    return wp.astype(BF16), _pad_heads(b_gate, GLA_HEADS, GLA_DK, GLA_DKP).reshape(2, 1, GLA_QKP)


def _per_seq(t, batch):
    n, ch = t.shape
    return t.reshape(batch, n // batch, ch)


def _to_col_major(t):
    batch, n, ch = t.shape
    return t.reshape(batch, n // GRID_W, GRID_W, ch).transpose(0, 2, 1, 3).reshape(batch, n, ch)


def _from_col_major(t):
    batch, n, ch = t.shape
    return t.reshape(batch, GRID_W, n // GRID_W, ch).transpose(0, 2, 1, 3).reshape(batch, n, ch)


def _to_time_major(t, batch):
    n, ch = t.shape
    return t.reshape(batch, n // batch, ch).transpose(1, 0, 2)


def _from_time_major(t):
    seq, batch, ch = t.shape
    return t.transpose(1, 0, 2).reshape(seq * batch, ch)


def kernel(x, c, ctx, c_ctx, ada_w, ada_b, norm_g, w_in, w_out, ff_w_gate, ff_w_up, ff_w_down, gla_w_gate, gla_b_gate, gla_norm_g, s5_a_re, s5_a_im, s5_log_dt, s5_b_re, s5_b_im, s5_c_re, s5_c_im, s5_d, s5_w_glu, s5_b_glu, ssd_conv_w, ssd_conv_b, ssd_dt_bias, ssd_a_log, ssd_d, ssd_norm_g, final_norm_g):
    batch, seq, _ = x.shape
    ctx_len = ctx.shape[1]
    depth = ada_w.shape[0]
    assert seq == GRID_W * CHUNK and ctx_len % TB_SCAN == 0 and seq % TM_TOKENS == 0
    assert batch == SUBLANE and (batch * ctx_len) % TM_TOKENS == 0 and batch % SCAN_BATCH == 0

    mod_rows = 2 * SUBLANE
    cc = jnp.zeros((mod_rows, D_MODEL), F32).at[:batch].set(c).at[batch].set(c_ctx)
    mods = _ada(cc, ada_w, ada_b).reshape(depth, mod_rows, N_MOD, D_MODEL)
    lat_mod = _mod_spec(seq // TM_TOKENS, None)
    ctx_mod = _mod_spec(None, batch)

    h = x.reshape(batch * seq, D_MODEL)
    hc = ctx.reshape(batch * ctx_len, D_MODEL)
    fg = final_norm_g.reshape(1, D_MODEL)
    wg, wu, wd = ff_w_gate.astype(BF16), ff_w_up.astype(BF16), ff_w_down.astype(BF16)
    w_all = _pack_w_in(w_in)
    wo = w_out.astype(BF16)
    consts = _scan_consts()

    for i in range(depth):
        ctx_out = i < depth - 1
        mod = mods[i]
        ng = norm_g[i]
        gla_wg, gla_bg = _pack_gla_gate(gla_w_gate[i], gla_b_gate[i])
        gla_ng = gla_norm_g[i].reshape(1, GLA_WIDTH)
        s5p = _s5_params(s5_a_re[i], s5_a_im[i], s5_log_dt[i], s5_b_re[i], s5_b_im[i], s5_c_re[i], s5_c_im[i],
                         s5_d[i], s5_w_glu[i], s5_b_glu[i])
        cw = jnp.pad(ssd_conv_w[i], ((0, SUBLANE - SSD_CONV), (0, 0)))
        cb = ssd_conv_b[i].reshape(1, SSD_XBC)
        dtb = _pad_last(ssd_dt_bias[i].reshape(1, 2 * SSD_HEADS), P_DT)
        alog = _pad_last(ssd_a_log[i].reshape(1, 2 * SSD_HEADS), P_DT)
        dsk = jnp.repeat(ssd_d[i], SSD_HEADDIM).reshape(1, SSD_WIDTH)
        ssd_ng = ssd_norm_g[i].reshape(1, SSD_WIDTH)

        h = _ffn1(h, mod, lat_mod, ng, wg, wu, wd, i)
        hc = _ffn1(hc, mod, ctx_mod, ng, wg, wu, wd, i)

        qk_l, v_l, r_l, glr_l, s5_l, z_l, xbc_l, dt_l = _inproj(h, mod, lat_mod, ng, w_all, i)
        qk_c, v_c, r_c, glr_c, s5_c, z_c, xbc_c, dt_c = _inproj(hc, mod, ctx_mod, ng, w_all, i)

        gla_s0 = jnp.zeros((batch, 2, GLA_WIDTH, GLA_QKP), F32)
        seqs = functools.partial(_per_seq, batch=batch)
        yg_c, gla_s = _gla(seqs(qk_c), seqs(v_c), seqs(r_c), seqs(glr_c), gla_s0, gla_wg, gla_bg, gla_ng, consts)
        yg_l, _ = _gla(seqs(qk_l), seqs(v_l), seqs(r_l), seqs(glr_l), gla_s, gla_wg, gla_bg, gla_ng, consts)
        yg_c = yg_c.reshape(-1, GLA_WIDTH)
        yg_l = yg_l.reshape(-1, GLA_WIDTH)

        s5_x0 = jnp.zeros((2, batch, 2 * S5_LANES), F32)
        ys_c, s5_x = _s5(_to_time_major(s5_c, batch), s5_x0, s5p)
        ys_l, _ = _s5(_to_time_major(s5_l, batch), s5_x, s5p)
        ys_l = _from_time_major(ys_l)

        ssd_s0 = jnp.zeros((batch, 2, SSD_BC, SSD_WIDTH), F32)
        yc_c, ssd_s = _ssd(seqs(z_c), seqs(xbc_c), seqs(dt_c), ssd_s0, cw, cb, dtb, alog, dsk, ssd_ng, consts)
        yc_l, _ = _ssd(_to_col_major(seqs(z_l)), _to_col_major(seqs(xbc_l)), _to_col_major(seqs(dt_l)),
                       ssd_s, cw, cb, dtb, alog, dsk, ssd_ng, consts)
        yc_c = yc_c.reshape(-1, SSD_WIDTH)
        yc_l = _from_col_major(yc_l).reshape(-1, SSD_WIDTH)

        h = _mix_ffn2(h, yg_l, ys_l, yc_l, mod, lat_mod, ng, wo, wg, wu, wd, fg, i, final=not ctx_out)
        if ctx_out:
            hc = _mix_ffn2(hc, yg_c, _from_time_major(ys_c), yc_c, mod, ctx_mod, ng, wo, wg, wu, wd, fg, i,
                           final=False)
    return h.reshape(batch, seq, D_MODEL)
```

```python
import functools
import math

import jax
import jax.numpy as jnp
from jax import lax
from jax.experimental import pallas as pl
from jax.experimental.pallas import tpu as pltpu

F32 = jnp.float32
BF16 = jnp.bfloat16
HI = lax.Precision.HIGHEST

D_MODEL = 1024
GRID_W = 64
N_MOD = 9
D_FF = 2816
EPS = 1e-6

GLA_HEADS = 4
GLA_DK = 48
GLA_DKP = 64
GLA_DV = 96
GLA_RANK = 16
GLA_TAU = 16.0
GLA_QK = GLA_HEADS * GLA_DK
GLA_QKP = GLA_HEADS * GLA_DKP
GLA_WIDTH = GLA_HEADS * GLA_DV

S5_WIDTH = 256
S5_GROUP = 16
S5_GROUPS = 16
S5_STATE = 64
S5_LANES = S5_GROUPS * S5_STATE
S5_STREAMS = 4
S5_HALF = S5_LANES // S5_STREAMS
S5_PARTS = 2

SSD_HEADS = 6
SSD_HEADDIM = 64
SSD_GROUPS = 2
SSD_REP = SSD_HEADS // SSD_GROUPS
SSD_STATE = 128
SSD_CONV = 5
SSD_WIDTH = SSD_HEADS * SSD_HEADDIM
SSD_BC = SSD_GROUPS * SSD_STATE
SSD_XBC = SSD_WIDTH + 2 * SSD_BC

MIX_WIDTH = GLA_WIDTH + S5_WIDTH + SSD_WIDTH
IN_SPLITS = (GLA_QK, GLA_QK, GLA_WIDTH, GLA_WIDTH, 2 * GLA_RANK, S5_WIDTH, SSD_WIDTH, SSD_XBC, 2 * SSD_HEADS)

CHUNK = 64
LANE = 128
SUBLANE = 8
HALO = SUBLANE

P_QK = 2 * GLA_QKP
P_GLR = LANE
P_DT = LANE
P_TOTAL = P_QK + 2 * GLA_WIDTH + P_GLR + S5_WIDTH + SSD_WIDTH + SSD_XBC + P_DT

TM_TOKENS = 512
TB_SCAN = 256
SCAN_BATCH = 2
TT_S5 = 128
VMEM_LIMIT = 56 * 1024 * 1024


def _silu(x):
    return x * jax.nn.sigmoid(x)


def _rms(x, g):
    return x * lax.rsqrt(jnp.mean(x * x, axis=-1, keepdims=True) + EPS) * g


def _dot(a, b):
    return jnp.dot(a.astype(BF16), b.astype(BF16), preferred_element_type=F32)


def _dot_nt(a, b):
    return lax.dot_general(a.astype(BF16), b.astype(BF16), (((1,), (1,)), ((), ())), preferred_element_type=F32)


def _dot_tn(a, b):
    return lax.dot_general(a.astype(BF16), b.astype(BF16), (((0,), (0,)), ((), ())), preferred_element_type=F32)


def _dot_hi(a, b):
    return jnp.dot(a, b, precision=HI, preferred_element_type=F32)


def _split3(x, pieces=3):
    out = []
    rest = x
    for _ in range(pieces):
        part = rest.astype(BF16)
        out.append(part)
        rest = rest - part.astype(F32)
    return tuple(out)


def _dot_exact_rhs(x, m3, pieces=3):
    return jnp.dot(jnp.concatenate(_split3(x, pieces), axis=1), m3, preferred_element_type=F32)


def _dot_exact_lhs(m3, x):
    return jnp.dot(m3, jnp.concatenate(_split3(x), axis=0), preferred_element_type=F32)


_DONE = object()


def _interleave(*streams):
    live = list(streams)
    while live:
        for stream in tuple(live):
            if next(stream, _DONE) is _DONE:
                live.remove(stream)


def _lagged(stream, stages):
    for _ in range(stages):
        yield
    yield from stream


def _iota(shape, dim):
    return lax.broadcasted_iota(jnp.int32, shape, dim)


def _block_id(shape, dim, width):
    idx = _iota(shape, dim)
    out = jnp.zeros(shape, F32)
    for k in range(1, -(-shape[dim] // width)):
        out = out + jnp.where(idx >= k * width, 1.0, 0.0)
    return out


def _chunk_tri(n, lower):
    i = _iota((n, n), 0)
    j = _iota((n, n), 1)
    same_chunk = (i & -CHUNK) == (j & -CHUNK)
    return jnp.where(same_chunk & ((j <= i) if lower else (j >= i)), 1.0, 0.0).astype(BF16)


def _ada_kernel(c_ref, w_ref, b_ref, o_ref):
    o_ref[0] = _dot_hi(_silu(c_ref[...]), w_ref[0]) + b_ref[0]


def _ada(cc, ada_w, ada_b):
    depth = ada_w.shape[0]
    rows = cc.shape[0]
    tn = D_MODEL
    return pl.pallas_call(
        _ada_kernel,
        grid=(depth, N_MOD * D_MODEL // tn),
        in_specs=[pl.BlockSpec((rows, D_MODEL), lambda l, n: (0, 0)),
                  pl.BlockSpec((1, D_MODEL, tn), lambda l, n: (l, 0, n)),
                  pl.BlockSpec((1, 1, tn), lambda l, n: (l, 0, n))],
        out_specs=pl.BlockSpec((1, rows, tn), lambda l, n: (l, 0, n)),
        out_shape=jax.ShapeDtypeStruct((depth, rows, N_MOD * D_MODEL), F32),
        name="ada_mod",
    )(cc, ada_w, ada_b.reshape(depth, 1, N_MOD * D_MODEL))


def _ffn_body(x, m, g, wg_ref, wu_ref, wd_ref, base):
    u = (_rms(x, g) * (1.0 + m[base + 1:base + 2]) + m[base:base + 1]).astype(BF16)
    gate = jnp.dot(u, wg_ref[...], preferred_element_type=F32)
    up = jnp.dot(u, wu_ref[...], preferred_element_type=F32)
    act = (_silu(gate) * up).astype(BF16)
    y = jnp.dot(act, wd_ref[...], preferred_element_type=F32)
    return x + 0.5 * m[base + 2:base + 3] * y


def _ffn1_kernel(h_ref, mod_ref, ng_ref, wg_ref, wu_ref, wd_ref, o_ref):
    o_ref[...] = _ffn_body(h_ref[...], mod_ref[0], ng_ref[0:1], wg_ref, wu_ref, wd_ref, 0)


def _mix_ffn2_kernel(h_ref, yg_ref, ys_ref, yc_ref, mod_ref, ng_ref, wo_ref,
                     wg_ref, wu_ref, wd_ref, fg_ref, o_ref, *, final):
    m = mod_ref[0]
    mixed = jnp.concatenate([yg_ref[...], ys_ref[...], yc_ref[...]], axis=1).astype(BF16)
    mix = jnp.dot(mixed, wo_ref[...], preferred_element_type=F32)
    x = h_ref[...] + m[5:6] * mix
    x = _ffn_body(x, m, ng_ref[2:3], wg_ref, wu_ref, wd_ref, 6)
    if final:
        x = _rms(x, fg_ref[...])
    o_ref[...] = x


def _const_spec(shape):
    nd = len(shape)
    return pl.BlockSpec(shape, lambda t: (0,) * nd, pipeline_mode=pl.Buffered(1))


def _layer_spec(stacked, *lead):
    tail = stacked.shape[len(lead):]
    return pl.BlockSpec((None,) * len(lead) + tail, lambda t: lead + (0,) * len(tail),
                        pipeline_mode=pl.Buffered(1))


def _tile_spec(width):
    return pl.BlockSpec((TM_TOKENS, width), lambda t: (t, 0))


def _mod_spec(tiles_per_row, fixed_row):
    if fixed_row is None:
        return pl.BlockSpec((1, N_MOD, D_MODEL), lambda t: (t // tiles_per_row, 0, 0))
    return pl.BlockSpec((1, N_MOD, D_MODEL), lambda t: (fixed_row, 0, 0))


def _dense_params():
    return pltpu.CompilerParams(dimension_semantics=("arbitrary",), vmem_limit_bytes=VMEM_LIMIT)


def _ffn1(h, mod, mod_spec, ng, wg, wu, wd, layer):
    n = h.shape[0]
    return pl.pallas_call(
        _ffn1_kernel,
        grid=(n // TM_TOKENS,),
        in_specs=[_tile_spec(D_MODEL), mod_spec, _const_spec(ng.shape),
                  _layer_spec(wg, layer, 0), _layer_spec(wu, layer, 0), _layer_spec(wd, layer, 0)],
        out_specs=_tile_spec(D_MODEL),
        out_shape=jax.ShapeDtypeStruct(h.shape, F32),
        compiler_params=_dense_params(),
        name="ffn1",
    )(h, mod, ng, wg, wu, wd)


def _mix_ffn2(h, yg, ys, yc, mod, mod_spec, ng, wo, wg, wu, wd, fg, layer, final):
    n = h.shape[0]
    return pl.pallas_call(
        functools.partial(_mix_ffn2_kernel, final=final),
        grid=(n // TM_TOKENS,),
        in_specs=[_tile_spec(D_MODEL), _tile_spec(GLA_WIDTH), _tile_spec(S5_WIDTH), _tile_spec(SSD_WIDTH),
                  mod_spec, _const_spec(ng.shape), _layer_spec(wo, layer),
                  _layer_spec(wg, layer, 1), _layer_spec(wu, layer, 1), _layer_spec(wd, layer, 1),
                  _const_spec(fg.shape)],
        out_specs=_tile_spec(D_MODEL),
        out_shape=jax.ShapeDtypeStruct(h.shape, F32),
        compiler_params=_dense_params(),
        name="mix_ffn2",
    )(h, yg, ys, yc, mod, ng, wo, wg, wu, wd, fg)


_P_WIDTHS = (P_QK, GLA_WIDTH, GLA_WIDTH, P_GLR, S5_WIDTH, SSD_WIDTH, SSD_XBC, P_DT)


def _inproj_kernel(h_ref, mod_ref, ng_ref, w_ref, *out_refs):
    m = mod_ref[0]
    u = _rms(h_ref[...], ng_ref[1:2]) * (1.0 + m[4:5]) + m[3:4]
    p = jnp.dot(u.astype(BF16), w_ref[...], preferred_element_type=F32)
    off = 0
    for ref, width in zip(out_refs, _P_WIDTHS):
        ref[...] = p[:, off:off + width]
        off += width


def _inproj(h, mod, mod_spec, ng, w_all, layer):
    n = h.shape[0]
    return pl.pallas_call(
        _inproj_kernel,
        grid=(n // TM_TOKENS,),
        in_specs=[_tile_spec(D_MODEL), mod_spec, _const_spec(ng.shape), _layer_spec(w_all, layer)],
        out_specs=[_tile_spec(w) for w in _P_WIDTHS],
        out_shape=[jax.ShapeDtypeStruct((n, w), F32) for w in _P_WIDTHS],
        compiler_params=_dense_params(),
        name="inproj",
    )(h, mod, ng, w_all)


def _scan_blk(ph, j, nblk):
    return jnp.where(ph == 0, nblk - 1 - j, j)


def _seq_spec(width, nblk):
    return pl.BlockSpec((SCAN_BATCH, TB_SCAN, width), lambda g, ph, j: (g, _scan_blk(ph, j, nblk), 0))


def _phase0_blk(ph, j, nblk):
    return jnp.where(ph == 0, nblk - 1 - j, 0)


def _seq_phase0_spec(width, nblk):
    return pl.BlockSpec((SCAN_BATCH, TB_SCAN, width), lambda g, ph, j: (g, _phase0_blk(ph, j, nblk), 0))


def _seq_phase1_spec(width, nblk):
    return pl.BlockSpec((SCAN_BATCH, TB_SCAN, width), lambda g, ph, j: (g, jnp.where(ph == 0, 0, j), 0))


def _state_spec(shape):
    return pl.BlockSpec((SCAN_BATCH,) + shape, lambda g, ph, j: (g,) + (0,) * len(shape))


def _param_spec(shape):
    nd = len(shape)
    return pl.BlockSpec(shape, lambda b, ph, j: (0,) * nd, pipeline_mode=pl.Buffered(1))


def _scan_consts():
    tri3 = jnp.stack([jnp.concatenate([_chunk_tri(TB_SCAN, lower=(d == 0))] * 3, axis=1) for d in range(2)])
    same_head = (_block_id((GLA_WIDTH, GLA_WIDTH), 0, GLA_DV)
                 == _block_id((GLA_WIDTH, GLA_WIDTH), 1, GLA_DV)).astype(BF16)
    gla_block = (_block_id((GLA_WIDTH, GLA_QKP), 0, GLA_DV)
                 == _block_id((GLA_WIDTH, GLA_QKP), 1, GLA_DKP)).astype(F32)
    expand = [jnp.where(_iota((LANE, SSD_WIDTH), 0) == d * SSD_HEADS + _block_id((LANE, SSD_WIDTH), 1, SSD_HEADDIM),
                        1.0, 0.0).astype(BF16) for d in range(2)]
    return {
        "tri3": tri3,
        "same_head2": jnp.concatenate([same_head] * 2, axis=0),
        "gla_block": gla_block,
        "expand3": jnp.concatenate([jnp.concatenate([e] * 3, axis=0) for e in expand], axis=1),
    }


def _scan_params():
    return pltpu.CompilerParams(dimension_semantics=("arbitrary", "arbitrary", "arbitrary"),
                                vmem_limit_bytes=VMEM_LIMIT)


def _gla_kernel(qk_ref, v_ref, r_ref, glr_ref, s0_ref, wg_ref, bg_ref, ng_ref, tri_ref, sameh_ref, bmask_ref,
                y_ref, sfin_ref, st_ref, ob_ref, qkb_ref, *, nblk):
    ph = pl.program_id(1)
    j = pl.program_id(2)
    nch = TB_SCAN // CHUNK
    scale = GLA_DK ** -0.5

    lane_k = _block_id((1, GLA_QKP), 1, GLA_DKP)
    lane_v = _block_id((1, GLA_WIDTH), 1, GLA_DV)
    kmask16 = [(lane_k == h).astype(BF16) for h in range(GLA_HEADS)]
    vmask16 = [(lane_v == h).astype(BF16) for h in range(GLA_HEADS)]
    blockmask = bmask_ref[...]

    def decay_cum(s, d):
        g = _dot(glr_ref[s], wg_ref[d]) + bg_ref[d]
        return _dot_exact_lhs(tri_ref[d], jax.nn.log_sigmoid(g) * (1.0 / GLA_TAU))

    def rows(x, c):
        return x[c * CHUNK:(c + 1) * CHUNK]

    def per_head_rows(x, masks):
        return jnp.concatenate([x * m for m in masks], axis=0)

    def backward_sweep(s):
        base = pl.multiple_of((nblk - 1 - j) * TB_SCAN, TB_SCAN)
        cum = decay_cum(s, 1)
        yield
        q_dec = (qk_ref[s, :, :GLA_QKP] * scale * jnp.exp(cum)).astype(BF16)
        k_inv = qk_ref[s, :, GLA_QKP:] * jnp.exp(-cum)
        v = v_ref[s]
        qkb_ref[s, pl.ds(base, TB_SCAN), :GLA_QKP] = q_dec
        qkb_ref[s, pl.ds(base, TB_SCAN), GLA_QKP:] = k_inv.astype(BF16)
        state = st_ref[s]
        yield
        for c in reversed(range(nch)):
            ob_ref[s, pl.ds(base + c * CHUNK, CHUNK), :] = _dot_nt(rows(q_dec, c), state)
            yield
            decay = jnp.exp(rows(cum, c)[0:1, :])
            contrib = _dot_tn(rows(v, c), rows(k_inv, c) * decay)
            yield
            state = state * decay + contrib * blockmask
            yield
        st_ref[s] = state

    def forward_sweep(s):
        base = pl.multiple_of(j * TB_SCAN, TB_SCAN)
        row_i = _iota((CHUNK, GLA_HEADS * CHUNK), 0)
        lane_j = _iota((CHUNK, GLA_HEADS * CHUNK), 1) & (CHUNK - 1)
        cum_f = decay_cum(s, 0)
        yield
        q_f = qk_ref[s, :, :GLA_QKP] * scale * jnp.exp(cum_f)
        yield
        k_f = qk_ref[s, :, GLA_QKP:] * jnp.exp(-cum_f)
        k_f16 = k_f.astype(BF16)
        yield
        v = v_ref[s]
        v16 = v.astype(BF16)
        q_b = qkb_ref[s, pl.ds(base, TB_SCAN), :GLA_QKP]
        k_b = qkb_ref[s, pl.ds(base, TB_SCAN), GLA_QKP:]
        state = st_ref[s]
        outs = []
        yield
        for c in range(nch):
            a_f = _dot_nt(rows(q_f, c), per_head_rows(rows(k_f16, c), kmask16))
            yield
            a_b = _dot_nt(rows(q_b, c), per_head_rows(rows(k_b, c), kmask16))
            yield
            scores = jnp.where(lane_j <= row_i, a_f, 0.0) + jnp.where(lane_j >= row_i, a_b, 0.0)
            o = _dot(scores, per_head_rows(rows(v16, c), vmask16))
            yield
            o = o + _dot_nt(rows(q_f, c), state)
            outs.append(o)
            yield
            decay = jnp.exp(rows(cum_f, c)[CHUNK - 1:CHUNK, :])
            contrib = _dot_tn(rows(v, c), rows(k_f, c) * decay)
            yield
            state = state * decay + contrib * blockmask
            yield
        st_ref[s] = state
        o = jnp.concatenate(outs, axis=0) + ob_ref[s, pl.ds(base, TB_SCAN), :]
        mean_sq = _dot_exact_rhs(o * o, sameh_ref[...], pieces=2) * (1.0 / GLA_DV)
        yield
        y_ref[s] = o * lax.rsqrt(mean_sq + EPS) * ng_ref[...] * _silu(r_ref[s])

    @pl.when(j == 0)
    def _():
        st_ref[...] = jnp.where(ph == 0, s0_ref[:, 1], s0_ref[:, 0])

    @pl.when(ph == 0)
    def _():
        _interleave(*(backward_sweep(s) for s in range(SCAN_BATCH)))

    @pl.when(ph == 1)
    def _():
        _interleave(*(forward_sweep(s) for s in range(SCAN_BATCH)))

    @pl.when(j == nblk - 1)
    def _():
        @pl.when(ph == 0)
        def _():
            sfin_ref[:, 1] = st_ref[...]

        @pl.when(ph == 1)
        def _():
            sfin_ref[:, 0] = st_ref[...]


def _gla(qk, v, r, glr, s0, wg, bg, ng, consts):
    batch, seq, _ = qk.shape
    nblk = seq // TB_SCAN
    sshape = (2, GLA_WIDTH, GLA_QKP)
    tri3, same_head2, gla_block = consts["tri3"], consts["same_head2"], consts["gla_block"]
    return pl.pallas_call(
        functools.partial(_gla_kernel, nblk=nblk),
        grid=(batch // SCAN_BATCH, 2, nblk),
        in_specs=[_seq_spec(P_QK, nblk), _seq_spec(GLA_WIDTH, nblk), _seq_phase1_spec(GLA_WIDTH, nblk),
                  _seq_spec(P_GLR, nblk), _state_spec(sshape),
                  _param_spec(wg.shape), _param_spec(bg.shape), _param_spec(ng.shape),
                  _param_spec(tri3.shape), _param_spec(same_head2.shape), _param_spec(gla_block.shape)],
        out_specs=[_seq_phase1_spec(GLA_WIDTH, nblk), _state_spec(sshape)],
        out_shape=[jax.ShapeDtypeStruct((batch, seq, GLA_WIDTH), F32),
                   jax.ShapeDtypeStruct((batch,) + sshape, F32)],
        scratch_shapes=[pltpu.VMEM((SCAN_BATCH, GLA_WIDTH, GLA_QKP), F32),
                        pltpu.VMEM((SCAN_BATCH, seq, GLA_WIDTH), F32),
                        pltpu.VMEM((SCAN_BATCH, seq, P_QK), BF16)],
        compiler_params=_scan_params(),
        name="gla",
    )(qk, v, r, glr, s0, wg, bg, ng, tri3, same_head2, gla_block)


def _ssd_kernel(z_ref, x_ref, xp_ref, xn_ref, dt_ref, s0_ref, cw_ref, cb_ref, dtb_ref, alog_ref, dsk_ref, ng_ref,
                tri_ref, exp_ref, y_ref, sfin_ref, st_ref, yb_ref, ext_ref, xs_ref, bc_ref, *, nblk):
    ph = pl.program_id(1)
    j = pl.program_id(2)
    blk = _scan_blk(ph, j, nblk)
    base = pl.multiple_of(blk * TB_SCAN, TB_SCAN)
    nch = TB_SCAN // CHUNK

    lane_h = _block_id((1, SSD_WIDTH), 1, SSD_HEADDIM)
    lane_g = _block_id((1, SSD_WIDTH), 1, SSD_HEADDIM * SSD_REP)
    lane_n = _block_id((1, SSD_BC), 1, SSD_STATE)
    head_lanes = [(lane_h == h).astype(BF16) for h in range(SSD_HEADS)]
    group_lanes = [(lane_g == g).astype(BF16) for g in range(SSD_GROUPS)]
    group_states = [(lane_n == g).astype(BF16) for g in range(SSD_GROUPS)]

    def rows(x, c):
        return x[c * CHUNK:(c + 1) * CHUNK]

    def conv_silu(s):
        ext_ref[s, 0:HALO, :] = jnp.where(blk > 0, xp_ref[s], 0.0)
        ext_ref[s, HALO:HALO + TB_SCAN, :] = x_ref[s]
        ext_ref[s, HALO + TB_SCAN:, :] = jnp.where(blk < nblk - 1, xn_ref[s], 0.0)
        pad = (SSD_CONV - 1) // 2
        acc = cb_ref[...]
        for t in range(SSD_CONV):
            acc = acc + ext_ref[s, pl.ds(HALO - pad + t, TB_SCAN), :] * cw_ref[t:t + 1, :]
        return _silu(acc)

    def decay_inputs(s):
        dt_c = jax.nn.softplus(dt_ref[s] + dtb_ref[...])
        return dt_c, dt_c * -jnp.exp(alog_ref[...])

    def expanded(d, cum_c, dt_c):
        both = _dot_exact_rhs(jnp.concatenate([cum_c, dt_c], axis=0), exp_ref[:, d * SSD_WIDTH:(d + 1) * SSD_WIDTH])
        return both[:TB_SCAN], both[TB_SCAN:]

    def expanded_both(cum_cf, cum_cb, dt_c):
        cum_c = jnp.where(_iota((TB_SCAN, LANE), 1) < SSD_HEADS, cum_cf, cum_cb)
        both = _dot_exact_rhs(jnp.concatenate([cum_c, dt_c], axis=0), exp_ref[...])
        return (both[:TB_SCAN, :SSD_WIDTH], both[TB_SCAN:, :SSD_WIDTH],
                both[:TB_SCAN, SSD_WIDTH:], both[TB_SCAN:, SSD_WIDTH:])

    def scan_terms(d, cum_e, dt_e):
        cum = cum_e.reshape(nch, CHUNK, SSD_WIDTH)
        last = cum[:, CHUNK - 1:CHUNK, :] if d == 0 else cum[:, 0:1, :]
        weight = jnp.exp(last - cum).reshape(TB_SCAN, SSD_WIDTH) * dt_e
        return jnp.exp(cum_e), weight, jnp.exp(last)

    def advance(state, decay, bm_c, xw_c):
        xw16 = xw_c.astype(BF16)
        contrib = [_dot_tn(bm_c[:, g * SSD_STATE:(g + 1) * SSD_STATE], xw16 * group_lanes[g])
                   for g in range(SSD_GROUPS)]
        return state * decay + jnp.concatenate(contrib, axis=0)

    def backward_sweep(s):
        xbc = conv_silu(s)
        yield
        dt_c, la_c = decay_inputs(s)
        cum_c = _dot_exact_lhs(tri_ref[1], la_c)
        yield
        xs = xbc[:, :SSD_WIDTH]
        bc16 = xbc[:, SSD_WIDTH:].astype(BF16)
        xs_ref[s, pl.ds(base, TB_SCAN), :] = xs
        bc_ref[s, pl.ds(base, TB_SCAN), :] = bc16
        bm, cm = bc16[:, :SSD_BC], bc16[:, SSD_BC:]
        cum_e, dt_e = expanded(1, cum_c, dt_c)
        yield
        grow, weight, decay = scan_terms(1, cum_e, dt_e)
        xw = xs * weight
        state = st_ref[s]
        outs = [None] * nch
        yield
        for c in reversed(range(nch)):
            outs[c] = _dot(rows(cm, c), state)
            state = advance(state, decay[c], rows(bm, c), rows(xw, c))
            yield
        st_ref[s] = state
        yb_ref[s, pl.ds(base, TB_SCAN), :] = jnp.concatenate(outs, axis=0) * grow

    def forward_sweep(s):
        xs = xs_ref[s, pl.ds(base, TB_SCAN), :]
        bm = bc_ref[s, pl.ds(base, TB_SCAN), :SSD_BC]
        cm = bc_ref[s, pl.ds(base, TB_SCAN), SSD_BC:]
        dt_c, la_c = decay_inputs(s)
        pieces = jnp.concatenate(_split3(la_c), axis=0)
        cum_cf = jnp.dot(tri_ref[0], pieces, preferred_element_type=F32)
        cum_cb = jnp.dot(tri_ref[1], pieces, preferred_element_type=F32)
        yield
        cum_f, dt_f, cum_b, dt_b = expanded_both(cum_cf, cum_cb, dt_c)
        yield
        grow, weight, decay = scan_terms(0, cum_f, dt_f)
        xw = xs * weight
        ci = _iota((CHUNK, SSD_WIDTH), 0)
        cj = _iota((CHUNK, SSD_WIDTH), 1) & (CHUNK - 1)
        diag = (_iota((TB_SCAN, SSD_WIDTH), 1) & (CHUNK - 1)) == (_iota((TB_SCAN, SSD_WIDTH), 0) & (CHUNK - 1))

        def at_j(x):
            return jnp.sum(jnp.where(diag, x, 0.0).reshape(nch, CHUNK, SSD_WIDTH), axis=1, keepdims=True)

        cum_row_f, cum_row_b, dt_row_f, dt_row_b = at_j(cum_f), at_j(cum_b), at_j(dt_f), at_j(dt_b)
        xs16 = xs.astype(BF16)
        state = st_ref[s]
        outs = []
        yield
        for c in range(nch):
            bm_group = [rows(bm, c) * group_states[g] for g in range(SSD_GROUPS)]
            bm_heads = jnp.concatenate([bm_group[h // SSD_REP] for h in range(SSD_HEADS)], axis=0)
            scores = _dot_nt(rows(cm, c), bm_heads)
            yield
            l_f = jnp.exp(jnp.where(cj <= ci, rows(cum_f, c) - cum_row_f[c], -jnp.inf))
            l_b = jnp.exp(jnp.where(cj >= ci, rows(cum_b, c) - cum_row_b[c], -jnp.inf))
            mat = scores * (l_f * dt_row_f[c] + l_b * dt_row_b[c])
            x_heads = jnp.concatenate([rows(xs16, c) * head_lanes[h] for h in range(SSD_HEADS)], axis=0)
            outs.append(_dot(mat, x_heads) + _dot(rows(cm, c), state) * rows(grow, c))
            yield
            state = advance(state, decay[c], rows(bm, c), rows(xw, c))
            yield
        st_ref[s] = state
        y = jnp.concatenate(outs, axis=0) + yb_ref[s, pl.ds(base, TB_SCAN), :] + xs * dsk_ref[...]
        y_ref[s] = _rms(y * _silu(z_ref[s]), ng_ref[...])

    @pl.when(j == 0)
    def _():
        st_ref[...] = jnp.where(ph == 0, s0_ref[:, 1], s0_ref[:, 0])

    @pl.when(ph == 0)
    def _():
        _interleave(*(backward_sweep(s) for s in range(SCAN_BATCH)))

    @pl.when(ph == 1)
    def _():
        _interleave(*(forward_sweep(s) for s in range(SCAN_BATCH)))

    @pl.when(j == nblk - 1)
    def _():
        @pl.when(ph == 0)
        def _():
            sfin_ref[:, 1] = st_ref[...]

        @pl.when(ph == 1)
        def _():
            sfin_ref[:, 0] = st_ref[...]


def _ssd(z, xbc, dt, s0, cw, cb, dtb, alog, dsk, ng, consts):
    batch, seq, _ = z.shape
    nblk = seq // TB_SCAN
    tiles = TB_SCAN // HALO
    sshape = (2, SSD_BC, SSD_WIDTH)

    def prev_map(g, ph, j):
        return (g, jnp.maximum(_phase0_blk(ph, j, nblk) * tiles - 1, 0), 0)

    def next_map(g, ph, j):
        return (g, jnp.minimum((_phase0_blk(ph, j, nblk) + 1) * tiles, seq // HALO - 1), 0)

    halo = (SCAN_BATCH, HALO, SSD_XBC)
    tri3, expand3 = consts["tri3"], consts["expand3"]
    return pl.pallas_call(
        functools.partial(_ssd_kernel, nblk=nblk),
        grid=(batch // SCAN_BATCH, 2, nblk),
        in_specs=[_seq_phase1_spec(SSD_WIDTH, nblk), _seq_phase0_spec(SSD_XBC, nblk),
                  pl.BlockSpec(halo, prev_map), pl.BlockSpec(halo, next_map),
                  _seq_spec(P_DT, nblk), _state_spec(sshape),
                  _param_spec(cw.shape), _param_spec(cb.shape), _param_spec(dtb.shape), _param_spec(alog.shape),
                  _param_spec(dsk.shape), _param_spec(ng.shape), _param_spec(tri3.shape), _param_spec(expand3.shape)],
        out_specs=[_seq_phase1_spec(SSD_WIDTH, nblk), _state_spec(sshape)],
        out_shape=[jax.ShapeDtypeStruct((batch, seq, SSD_WIDTH), F32),
                   jax.ShapeDtypeStruct((batch,) + sshape, F32)],
        scratch_shapes=[pltpu.VMEM((SCAN_BATCH, SSD_BC, SSD_WIDTH), F32),
                        pltpu.VMEM((SCAN_BATCH, seq, SSD_WIDTH), F32),
                        pltpu.VMEM((SCAN_BATCH, TB_SCAN + 2 * HALO, SSD_XBC), F32),
                        pltpu.VMEM((SCAN_BATCH, seq, SSD_WIDTH), F32),
                        pltpu.VMEM((SCAN_BATCH, seq, 2 * SSD_BC), BF16)],
        compiler_params=_scan_params(),
        name="ssd",
    )(z, xbc, xbc, xbc, dt, s0, cw, cb, dtb, alog, dsk, ng, tri3, expand3)


def _s5_kernel(*refs, reverse, final, batch):
    if final:
        (u_ref, x0_ref, bblk_ref, a_ref, cblk_ref, yprev_ref, d_ref, wglu_ref, bglu_ref,
         y_ref, xfin_ref, xs_ref, st_ref) = refs
    else:
        u_ref, x0_ref, bblk_ref, a_ref, cblk_ref, y_ref, xfin_ref, xs_ref, st_ref = refs
    i = pl.program_id(0)
    rows = TT_S5 * batch

    @pl.when(i == 0)
    def _():
        st_ref[...] = x0_ref[...]

    u = u_ref[...].reshape(rows, S5_WIDTH)
    u16 = u.astype(BF16)
    piece = 2 * S5_HALF // S5_PARTS
    readout = [None] * S5_STREAMS

    def stream(h):
        lo = h * 2 * S5_HALF
        re = slice(lo, lo + S5_HALF)
        im = slice(lo + S5_HALF, lo + 2 * S5_HALF)
        for p in range(S5_PARTS):
            cols = slice(lo + p * piece, lo + (p + 1) * piece)
            xs_ref[:, cols] = jnp.dot(u16, bblk_ref[:, cols], preferred_element_type=F32)
            yield
        a_re = jnp.broadcast_to(a_ref[0:1, h * S5_HALF:(h + 1) * S5_HALF], (batch, S5_HALF))
        a_im = jnp.broadcast_to(a_ref[1:2, h * S5_HALF:(h + 1) * S5_HALF], (batch, S5_HALF))
        x_re = st_ref[:, re]
        x_im = st_ref[:, im]
        for n in range(TT_S5):
            t = (TT_S5 - 1 - n) if reverse else n
            row = slice(t * batch, (t + 1) * batch)
            x_re, x_im = (a_re * x_re - a_im * x_im + xs_ref[row, re],
                          a_re * x_im + a_im * x_re + xs_ref[row, im])
            xs_ref[row, re] = x_re
            xs_ref[row, im] = x_im
            if (n + 1) % (TT_S5 // S5_PARTS) == 0:
                yield
        st_ref[:, re] = x_re
        st_ref[:, im] = x_im
        xfin_ref[:, re] = x_re
        xfin_ref[:, im] = x_im
        acc = None
        for p in range(S5_PARTS):
            cols = slice(lo + p * piece, lo + (p + 1) * piece)
            part = jnp.dot(xs_ref[:, cols].astype(BF16), cblk_ref[cols, :], preferred_element_type=F32)
            acc = part if acc is None else acc + part
            yield
        readout[h] = acc

    _interleave(*(_lagged(stream(h), h * S5_PARTS) for h in range(S5_STREAMS)))
    y = sum(readout[1:], readout[0])
    if final:
        y = y + yprev_ref[...].reshape(rows, S5_WIDTH) + u * d_ref[...]
        g = jax.nn.gelu(y)
        y = g * jax.nn.sigmoid(_dot(g, wglu_ref[...]) + bglu_ref[...])
    y_ref[...] = y.reshape(TT_S5, batch, S5_WIDTH)


def _s5_dir(u, x0, bblk, a, cblk, extra, reverse):
    seq, batch, _ = u.shape
    nblk = seq // TT_S5
    final = extra is not None

    def seq_map(i):
        return ((nblk - 1 - i) if reverse else i, 0, 0)

    def const(shape):
        nd = len(shape)
        return pl.BlockSpec(shape, lambda i: (0,) * nd)

    seq_spec = pl.BlockSpec((TT_S5, batch, S5_WIDTH), seq_map)
    in_specs = [seq_spec, const(x0.shape), const(bblk.shape), const(a.shape), const(cblk.shape)]
    args = [u, x0, bblk, a, cblk]
    if final:
        yprev, d, wglu, bglu = extra
        in_specs += [seq_spec, const(d.shape), const(wglu.shape), const(bglu.shape)]
        args += [yprev, d, wglu, bglu]
    return pl.pallas_call(
        functools.partial(_s5_kernel, reverse=reverse, final=final, batch=batch),
        grid=(nblk,),
        in_specs=in_specs,
        out_specs=[seq_spec, const(x0.shape)],
        out_shape=[jax.ShapeDtypeStruct(u.shape, F32), jax.ShapeDtypeStruct(x0.shape, F32)],
        scratch_shapes=[pltpu.VMEM((TT_S5 * batch, 2 * S5_LANES), F32), pltpu.VMEM(x0.shape, F32)],
        compiler_params=pltpu.CompilerParams(dimension_semantics=("arbitrary",), vmem_limit_bytes=VMEM_LIMIT),
        name="s5_bwd" if reverse else "s5_fwd",
    )(*args)


def _s5(u_tm, x0, prm):
    y_b, xf_b = _s5_dir(u_tm, x0[1], prm["bblk"][1], prm["a"][1], prm["cblk"], None, True)
    y, xf_f = _s5_dir(u_tm, x0[0], prm["bblk"][0], prm["a"][0], prm["cblk"],
                      (y_b, prm["d"], prm["wglu"], prm["bglu"]), False)
    return y, jnp.stack([xf_f, xf_b])


def _s5_params(a_re, a_im, log_dt, b_re, b_im, c_re, c_im, d, w_glu, b_glu):
    dt = jnp.exp(log_dt)[..., None]
    mag = jnp.exp(dt * a_re)
    ab_re = mag * jnp.cos(dt * a_im)
    ab_im = mag * jnp.sin(dt * a_im)
    den = a_re * a_re + a_im * a_im
    num_re = ab_re - 1.0
    num_im = ab_im
    f_re = (num_re * a_re + num_im * a_im) / den
    f_im = (num_im * a_re - num_re * a_im) / den
    bb_re = f_re[..., None] * b_re - f_im[..., None] * b_im
    bb_im = f_re[..., None] * b_im + f_im[..., None] * b_re
    eye = jnp.eye(S5_GROUPS, dtype=F32)

    def in_block(bb):
        return jnp.einsum("dgph,gk->dghkp", bb, eye).reshape(2, S5_WIDTH, S5_LANES)

    def out_block(c):
        return jnp.einsum("ghp,gk->gpkh", c, eye).reshape(S5_LANES, S5_WIDTH)

    def stream_major(re, im, axis):
        parts = []
        for h in range(S5_STREAMS):
            idx = [slice(None)] * re.ndim
            idx[axis] = slice(h * S5_HALF, (h + 1) * S5_HALF)
            parts += [re[tuple(idx)], im[tuple(idx)]]
        return jnp.concatenate(parts, axis=axis)

    return {
        "a": jnp.stack([ab_re.reshape(2, S5_LANES), ab_im.reshape(2, S5_LANES)], axis=1),
        "bblk": stream_major(in_block(bb_re), in_block(bb_im), 2).astype(BF16),
        "cblk": stream_major(out_block(c_re), -out_block(c_im), 0).astype(BF16),
        "d": d.reshape(1, S5_WIDTH),
        "wglu": w_glu.astype(BF16),
        "bglu": b_glu.reshape(1, S5_WIDTH),
    }


def _pad_heads(w, heads, width, padded):
    lead = w.shape[:-1]
    w = w.reshape(lead + (heads, width))
    w = jnp.pad(w, [(0, 0)] * len(lead) + [(0, 0), (0, padded - width)])
    return w.reshape(lead + (heads * padded,))


def _pad_last(w, width):
    return jnp.pad(w, [(0, 0)] * (w.ndim - 1) + [(0, width - w.shape[-1])])


def _pack_w_in(w_in):
    pts = []
    acc = 0
    for s in IN_SPLITS:
        pts.append((acc, acc + s))
        acc += s
    q, k, v, r, glr, s5, z, xbc, dt = (w_in[..., a:b] for a, b in pts)
    return jnp.concatenate([
        _pad_heads(q, GLA_HEADS, GLA_DK, GLA_DKP), _pad_heads(k, GLA_HEADS, GLA_DK, GLA_DKP), v, r,
        _pad_last(glr, P_GLR), s5, z, xbc, _pad_last(dt, P_DT)], axis=-1).astype(BF16)


def _pack_gla_gate(w_gate, b_gate):
    w = _pad_heads(w_gate, GLA_HEADS, GLA_DK, GLA_DKP)
    wp = jnp.zeros((2, P_GLR, GLA_QKP), F32)
    for d in range(2):
        wp = wp.at[d, d * GLA_RANK:(d + 1) * GLA_RANK].set(w[d])
    return wp.astype(BF16), _pad_heads(b_gate, GLA_HEADS, GLA_DK, GLA_DKP).reshape(2, 1, GLA_QKP)


def _per_seq(t, batch):
    n, ch = t.shape
    return t.reshape(batch, n // batch, ch)


def _to_col_major(t):
    batch, n, ch = t.shape
    return t.reshape(batch, n // GRID_W, GRID_W, ch).transpose(0, 2, 1, 3).reshape(batch, n, ch)


def _from_col_major(t):
    batch, n, ch = t.shape
    return t.reshape(batch, GRID_W, n // GRID_W, ch).transpose(0, 2, 1, 3).reshape(batch, n, ch)


def _to_time_major(t, batch):
    n, ch = t.shape
    return t.reshape(batch, n // batch, ch).transpose(1, 0, 2)


def _from_time_major(t):
    seq, batch, ch = t.shape
    return t.transpose(1, 0, 2).reshape(seq * batch, ch)


def kernel(x, c, ctx, c_ctx, ada_w, ada_b, norm_g, w_in, w_out, ff_w_gate, ff_w_up, ff_w_down, gla_w_gate, gla_b_gate, gla_norm_g, s5_a_re, s5_a_im, s5_log_dt, s5_b_re, s5_b_im, s5_c_re, s5_c_im, s5_d, s5_w_glu, s5_b_glu, ssd_conv_w, ssd_conv_b, ssd_dt_bias, ssd_a_log, ssd_d, ssd_norm_g, final_norm_g):
    batch, seq, _ = x.shape
    ctx_len = ctx.shape[1]
    depth = ada_w.shape[0]
    assert seq == GRID_W * CHUNK and ctx_len % TB_SCAN == 0 and seq % TM_TOKENS == 0
    assert batch == SUBLANE and (batch * ctx_len) % TM_TOKENS == 0 and batch % SCAN_BATCH == 0

    mod_rows = 2 * SUBLANE
    cc = jnp.zeros((mod_rows, D_MODEL), F32).at[:batch].set(c).at[batch].set(c_ctx)
    mods = _ada(cc, ada_w, ada_b).reshape(depth, mod_rows, N_MOD, D_MODEL)
    lat_mod = _mod_spec(seq // TM_TOKENS, None)
    ctx_mod = _mod_spec(None, batch)

    h = x.reshape(batch * seq, D_MODEL)
    hc = ctx.reshape(batch * ctx_len, D_MODEL)
    fg = final_norm_g.reshape(1, D_MODEL)
    wg, wu, wd = ff_w_gate.astype(BF16), ff_w_up.astype(BF16), ff_w_down.astype(BF16)
    w_all = _pack_w_in(w_in)
    wo = w_out.astype(BF16)
    consts = _scan_consts()

    for i in range(depth):
        ctx_out = i < depth - 1
        mod = mods[i]
        ng = norm_g[i]
        gla_wg, gla_bg = _pack_gla_gate(gla_w_gate[i], gla_b_gate[i])
        gla_ng = gla_norm_g[i].reshape(1, GLA_WIDTH)
        s5p = _s5_params(s5_a_re[i], s5_a_im[i], s5_log_dt[i], s5_b_re[i], s5_b_im[i], s5_c_re[i], s5_c_im[i],
                         s5_d[i], s5_w_glu[i], s5_b_glu[i])
        cw = jnp.pad(ssd_conv_w[i], ((0, SUBLANE - SSD_CONV), (0, 0)))
        cb = ssd_conv_b[i].reshape(1, SSD_XBC)
        dtb = _pad_last(ssd_dt_bias[i].reshape(1, 2 * SSD_HEADS), P_DT)
        alog = _pad_last(ssd_a_log[i].reshape(1, 2 * SSD_HEADS), P_DT)
        dsk = jnp.repeat(ssd_d[i], SSD_HEADDIM).reshape(1, SSD_WIDTH)
        ssd_ng = ssd_norm_g[i].reshape(1, SSD_WIDTH)

        h = _ffn1(h, mod, lat_mod, ng, wg, wu, wd, i)
        hc = _ffn1(hc, mod, ctx_mod, ng, wg, wu, wd, i)

        qk_l, v_l, r_l, glr_l, s5_l, z_l, xbc_l, dt_l = _inproj(h, mod, lat_mod, ng, w_all, i)
        qk_c, v_c, r_c, glr_c, s5_c, z_c, xbc_c, dt_c = _inproj(hc, mod, ctx_mod, ng, w_all, i)

        gla_s0 = jnp.zeros((batch, 2, GLA_WIDTH, GLA_QKP), F32)
        seqs = functools.partial(_per_seq, batch=batch)
        yg_c, gla_s = _gla(seqs(qk_c), seqs(v_c), seqs(r_c), seqs(glr_c), gla_s0, gla_wg, gla_bg, gla_ng, consts)
        yg_l, _ = _gla(seqs(qk_l), seqs(v_l), seqs(r_l), seqs(glr_l), gla_s, gla_wg, gla_bg, gla_ng, consts)
        yg_c = yg_c.reshape(-1, GLA_WIDTH)
        yg_l = yg_l.reshape(-1, GLA_WIDTH)

        s5_x0 = jnp.zeros((2, batch, 2 * S5_LANES), F32)
        ys_c, s5_x = _s5(_to_time_major(s5_c, batch), s5_x0, s5p)
        ys_l, _ = _s5(_to_time_major(s5_l, batch), s5_x, s5p)
        ys_l = _from_time_major(ys_l)

        ssd_s0 = jnp.zeros((batch, 2, SSD_BC, SSD_WIDTH), F32)
        yc_c, ssd_s = _ssd(seqs(z_c), seqs(xbc_c), seqs(dt_c), ssd_s0, cw, cb, dtb, alog, dsk, ssd_ng, consts)
        yc_l, _ = _ssd(_to_col_major(seqs(z_l)), _to_col_major(seqs(xbc_l)), _to_col_major(seqs(dt_l)),
                       ssd_s, cw, cb, dtb, alog, dsk, ssd_ng, consts)
        yc_c = yc_c.reshape(-1, SSD_WIDTH)
        yc_l = _from_col_major(yc_l).reshape(-1, SSD_WIDTH)

        h = _mix_ffn2(h, yg_l, ys_l, yc_l, mod, lat_mod, ng, wo, wg, wu, wd, fg, i, final=not ctx_out)
        if ctx_out:
            hc = _mix_ffn2(hc, yg_c, _from_time_major(ys_c), yc_c, mod, ctx_mod, ng, wo, wg, wu, wd, fg, i,
                           final=False)
    return h.reshape(batch, seq, D_MODEL)
```

```python
import functools
import math

import jax
import jax.numpy as jnp
from jax import lax
from jax.experimental import pallas as pl
from jax.experimental.pallas import tpu as pltpu

F32 = jnp.float32
BF16 = jnp.bfloat16
HI = lax.Precision.HIGHEST

D_MODEL = 1024
GRID_W = 64
N_MOD = 9
D_FF = 2816
EPS = 1e-6

GLA_HEADS = 4
GLA_DK = 48
GLA_DKP = 64
GLA_DV = 96
GLA_RANK = 16
GLA_TAU = 16.0
GLA_QK = GLA_HEADS * GLA_DK
GLA_QKP = GLA_HEADS * GLA_DKP
GLA_WIDTH = GLA_HEADS * GLA_DV

S5_WIDTH = 256
S5_GROUP = 16
S5_GROUPS = 16
S5_STATE = 64
S5_LANES = S5_GROUPS * S5_STATE
S5_STREAMS = 4
S5_HALF = S5_LANES // S5_STREAMS
S5_PARTS = 2

SSD_HEADS = 6
SSD_HEADDIM = 64
SSD_GROUPS = 2
SSD_REP = SSD_HEADS // SSD_GROUPS
SSD_STATE = 128
SSD_CONV = 5
SSD_WIDTH = SSD_HEADS * SSD_HEADDIM
SSD_BC = SSD_GROUPS * SSD_STATE
SSD_XBC = SSD_WIDTH + 2 * SSD_BC

MIX_WIDTH = GLA_WIDTH + S5_WIDTH + SSD_WIDTH
IN_SPLITS = (GLA_QK, GLA_QK, GLA_WIDTH, GLA_WIDTH, 2 * GLA_RANK, S5_WIDTH, SSD_WIDTH, SSD_XBC, 2 * SSD_HEADS)

CHUNK = 64
LANE = 128
SUBLANE = 8
HALO = SUBLANE

P_QK = 2 * GLA_QKP
P_GLR = LANE
P_DT = LANE
P_TOTAL = P_QK + 2 * GLA_WIDTH + P_GLR + S5_WIDTH + SSD_WIDTH + SSD_XBC + P_DT

TM_TOKENS = 512
TB_SCAN = 256
SCAN_BATCH = 2
TT_S5 = 128
VMEM_LIMIT = 56 * 1024 * 1024


def _silu(x):
    return x * jax.nn.sigmoid(x)


def _rms(x, g):
    return x * lax.rsqrt(jnp.mean(x * x, axis=-1, keepdims=True) + EPS) * g


def _dot(a, b):
    return jnp.dot(a.astype(BF16), b.astype(BF16), preferred_element_type=F32)


def _dot_nt(a, b):
    return lax.dot_general(a.astype(BF16), b.astype(BF16), (((1,), (1,)), ((), ())), preferred_element_type=F32)


def _dot_tn(a, b):
    return lax.dot_general(a.astype(BF16), b.astype(BF16), (((0,), (0,)), ((), ())), preferred_element_type=F32)


def _dot_hi(a, b):
    return jnp.dot(a, b, precision=HI, preferred_element_type=F32)


def _split3(x, pieces=3):
    out = []
    rest = x
    for _ in range(pieces):
        part = rest.astype(BF16)
        out.append(part)
        rest = rest - part.astype(F32)
    return tuple(out)


def _dot_exact_rhs(x, m3, pieces=3):
    return jnp.dot(jnp.concatenate(_split3(x, pieces), axis=1), m3, preferred_element_type=F32)


def _dot_exact_lhs(m3, x):
    return jnp.dot(m3, jnp.concatenate(_split3(x), axis=0), preferred_element_type=F32)


_DONE = object()


def _interleave(*streams):
    live = list(streams)
    while live:
        for stream in tuple(live):
            if next(stream, _DONE) is _DONE:
                live.remove(stream)


def _lagged(stream, stages):
    for _ in range(stages):
        yield
    yield from stream


def _iota(shape, dim):
    return lax.broadcasted_iota(jnp.int32, shape, dim)


def _block_id(shape, dim, width):
    idx = _iota(shape, dim)
    out = jnp.zeros(shape, F32)
    for k in range(1, -(-shape[dim] // width)):
        out = out + jnp.where(idx >= k * width, 1.0, 0.0)
    return out


def _chunk_tri(n, lower):
    i = _iota((n, n), 0)
    j = _iota((n, n), 1)
    same_chunk = (i & -CHUNK) == (j & -CHUNK)
    return jnp.where(same_chunk & ((j <= i) if lower else (j >= i)), 1.0, 0.0).astype(BF16)


def _ada_kernel(c_ref, w_ref, b_ref, o_ref):
    o_ref[0] = _dot_hi(_silu(c_ref[...]), w_ref[0]) + b_ref[0]


def _ada(cc, ada_w, ada_b):
    depth = ada_w.shape[0]
    rows = cc.shape[0]
    tn = D_MODEL
    return pl.pallas_call(
        _ada_kernel,
        grid=(depth, N_MOD * D_MODEL // tn),
        in_specs=[pl.BlockSpec((rows, D_MODEL), lambda l, n: (0, 0)),
                  pl.BlockSpec((1, D_MODEL, tn), lambda l, n: (l, 0, n)),
                  pl.BlockSpec((1, 1, tn), lambda l, n: (l, 0, n))],
        out_specs=pl.BlockSpec((1, rows, tn), lambda l, n: (l, 0, n)),
        out_shape=jax.ShapeDtypeStruct((depth, rows, N_MOD * D_MODEL), F32),
        name="ada_mod",
    )(cc, ada_w, ada_b.reshape(depth, 1, N_MOD * D_MODEL))


def _ffn_body(x, m, g, wg_ref, wu_ref, wd_ref, base):
    u = (_rms(x, g) * (1.0 + m[base + 1:base + 2]) + m[base:base + 1]).astype(BF16)
    gate = jnp.dot(u, wg_ref[...], preferred_element_type=F32)
    up = jnp.dot(u, wu_ref[...], preferred_element_type=F32)
    act = (_silu(gate) * up).astype(BF16)
    y = jnp.dot(act, wd_ref[...], preferred_element_type=F32)
    return x + 0.5 * m[base + 2:base + 3] * y


def _ffn1_kernel(h_ref, mod_ref, ng_ref, wg_ref, wu_ref, wd_ref, o_ref):
    o_ref[...] = _ffn_body(h_ref[...], mod_ref[0], ng_ref[0:1], wg_ref, wu_ref, wd_ref, 0)


def _mix_ffn2_kernel(h_ref, yg_ref, ys_ref, yc_ref, mod_ref, ng_ref, wo_ref,
                     wg_ref, wu_ref, wd_ref, fg_ref, o_ref, *, final):
    m = mod_ref[0]
    mixed = jnp.concatenate([yg_ref[...], ys_ref[...], yc_ref[...]], axis=1).astype(BF16)
    mix = jnp.dot(mixed, wo_ref[...], preferred_element_type=F32)
    x = h_ref[...] + m[5:6] * mix
    x = _ffn_body(x, m, ng_ref[2:3], wg_ref, wu_ref, wd_ref, 6)
    if final:
        x = _rms(x, fg_ref[...])
    o_ref[...] = x


def _const_spec(shape):
    nd = len(shape)
    return pl.BlockSpec(shape, lambda t: (0,) * nd, pipeline_mode=pl.Buffered(1))


def _layer_spec(stacked, *lead):
    tail = stacked.shape[len(lead):]
    return pl.BlockSpec((None,) * len(lead) + tail, lambda t: lead + (0,) * len(tail),
                        pipeline_mode=pl.Buffered(1))


def _tile_spec(width):
    return pl.BlockSpec((TM_TOKENS, width), lambda t: (t, 0))


def _mod_spec(tiles_per_row, fixed_row):
    if fixed_row is None:
        return pl.BlockSpec((1, N_MOD, D_MODEL), lambda t: (t // tiles_per_row, 0, 0))
    return pl.BlockSpec((1, N_MOD, D_MODEL), lambda t: (fixed_row, 0, 0))


def _dense_params():
    return pltpu.CompilerParams(dimension_semantics=("arbitrary",), vmem_limit_bytes=VMEM_LIMIT)


def _ffn1(h, mod, mod_spec, ng, wg, wu, wd, layer):
    n = h.shape[0]
    return pl.pallas_call(
        _ffn1_kernel,
        grid=(n // TM_TOKENS,),
        in_specs=[_tile_spec(D_MODEL), mod_spec, _const_spec(ng.shape),
                  _layer_spec(wg, layer, 0), _layer_spec(wu, layer, 0), _layer_spec(wd, layer, 0)],
        out_specs=_tile_spec(D_MODEL),
        out_shape=jax.ShapeDtypeStruct(h.shape, F32),
        compiler_params=_dense_params(),
        name="ffn1",
    )(h, mod, ng, wg, wu, wd)


def _mix_ffn2(h, yg, ys, yc, mod, mod_spec, ng, wo, wg, wu, wd, fg, layer, final):
    n = h.shape[0]
    return pl.pallas_call(
        functools.partial(_mix_ffn2_kernel, final=final),
        grid=(n // TM_TOKENS,),
        in_specs=[_tile_spec(D_MODEL), _tile_spec(GLA_WIDTH), _tile_spec(S5_WIDTH), _tile_spec(SSD_WIDTH),
                  mod_spec, _const_spec(ng.shape), _layer_spec(wo, layer),
                  _layer_spec(wg, layer, 1), _layer_spec(wu, layer, 1), _layer_spec(wd, layer, 1),
                  _const_spec(fg.shape)],
        out_specs=_tile_spec(D_MODEL),
        out_shape=jax.ShapeDtypeStruct(h.shape, F32),
        compiler_params=_dense_params(),
        name="mix_ffn2",
    )(h, yg, ys, yc, mod, ng, wo, wg, wu, wd, fg)


_P_WIDTHS = (P_QK, GLA_WIDTH, GLA_WIDTH, P_GLR, S5_WIDTH, SSD_WIDTH, SSD_XBC, P_DT)


def _inproj_kernel(h_ref, mod_ref, ng_ref, w_ref, *out_refs):
    m = mod_ref[0]
    u = _rms(h_ref[...], ng_ref[1:2]) * (1.0 + m[4:5]) + m[3:4]
    p = jnp.dot(u.astype(BF16), w_ref[...], preferred_element_type=F32)
    off = 0
    for ref, width in zip(out_refs, _P_WIDTHS):
        ref[...] = p[:, off:off + width]
        off += width


def _inproj(h, mod, mod_spec, ng, w_all, layer):
    n = h.shape[0]
    return pl.pallas_call(
        _inproj_kernel,
        grid=(n // TM_TOKENS,),
        in_specs=[_tile_spec(D_MODEL), mod_spec, _const_spec(ng.shape), _layer_spec(w_all, layer)],
        out_specs=[_tile_spec(w) for w in _P_WIDTHS],
        out_shape=[jax.ShapeDtypeStruct((n, w), F32) for w in _P_WIDTHS],
        compiler_params=_dense_params(),
        name="inproj",
    )(h, mod, ng, w_all)


def _scan_blk(ph, j, nblk):
    return jnp.where(ph == 0, nblk - 1 - j, j)


def _seq_spec(width, nblk):
    return pl.BlockSpec((SCAN_BATCH, TB_SCAN, width), lambda g, ph, j: (g, _scan_blk(ph, j, nblk), 0))


def _phase0_blk(ph, j, nblk):
    return jnp.where(ph == 0, nblk - 1 - j, 0)


def _seq_phase0_spec(width, nblk):
    return pl.BlockSpec((SCAN_BATCH, TB_SCAN, width), lambda g, ph, j: (g, _phase0_blk(ph, j, nblk), 0))


def _seq_phase1_spec(width, nblk):
    return pl.BlockSpec((SCAN_BATCH, TB_SCAN, width), lambda g, ph, j: (g, jnp.where(ph == 0, 0, j), 0))


def _state_spec(shape):
    return pl.BlockSpec((SCAN_BATCH,) + shape, lambda g, ph, j: (g,) + (0,) * len(shape))


def _param_spec(shape):
    nd = len(shape)
    return pl.BlockSpec(shape, lambda b, ph, j: (0,) * nd, pipeline_mode=pl.Buffered(1))


def _scan_consts():
    tri3 = jnp.stack([jnp.concatenate([_chunk_tri(TB_SCAN, lower=(d == 0))] * 3, axis=1) for d in range(2)])
    same_head = (_block_id((GLA_WIDTH, GLA_WIDTH), 0, GLA_DV)
                 == _block_id((GLA_WIDTH, GLA_WIDTH), 1, GLA_DV)).astype(BF16)
    gla_block = (_block_id((GLA_WIDTH, GLA_QKP), 0, GLA_DV)
                 == _block_id((GLA_WIDTH, GLA_QKP), 1, GLA_DKP)).astype(F32)
    expand = [jnp.where(_iota((LANE, SSD_WIDTH), 0) == d * SSD_HEADS + _block_id((LANE, SSD_WIDTH), 1, SSD_HEADDIM),
                        1.0, 0.0).astype(BF16) for d in range(2)]
    return {
        "tri3": tri3,
        "same_head2": jnp.concatenate([same_head] * 2, axis=0),
        "gla_block": gla_block,
        "expand3": jnp.concatenate([jnp.concatenate([e] * 3, axis=0) for e in expand], axis=1),
    }


def _scan_params():
    return pltpu.CompilerParams(dimension_semantics=("arbitrary", "arbitrary", "arbitrary"),
                                vmem_limit_bytes=VMEM_LIMIT)


def _gla_kernel(qk_ref, v_ref, r_ref, glr_ref, s0_ref, wg_ref, bg_ref, ng_ref, tri_ref, sameh_ref, bmask_ref,
                y_ref, sfin_ref, st_ref, ob_ref, qkb_ref, *, nblk):
    ph = pl.program_id(1)
    j = pl.program_id(2)
    nch = TB_SCAN // CHUNK
    scale = GLA_DK ** -0.5

    lane_k = _block_id((1, GLA_QKP), 1, GLA_DKP)
    lane_v = _block_id((1, GLA_WIDTH), 1, GLA_DV)
    kmask16 = [(lane_k == h).astype(BF16) for h in range(GLA_HEADS)]
    vmask16 = [(lane_v == h).astype(BF16) for h in range(GLA_HEADS)]
    blockmask = bmask_ref[...]

    def decay_cum(s, d):
        g = _dot(glr_ref[s], wg_ref[d]) + bg_ref[d]
        return _dot_exact_lhs(tri_ref[d], jax.nn.log_sigmoid(g) * (1.0 / GLA_TAU))

    def rows(x, c):
        return x[c * CHUNK:(c + 1) * CHUNK]

    def per_head_rows(x, masks):
        return jnp.concatenate([x * m for m in masks], axis=0)

    def backward_sweep(s):
        base = pl.multiple_of((nblk - 1 - j) * TB_SCAN, TB_SCAN)
        cum = decay_cum(s, 1)
        yield
        q_dec = (qk_ref[s, :, :GLA_QKP] * scale * jnp.exp(cum)).astype(BF16)
        k_inv = qk_ref[s, :, GLA_QKP:] * jnp.exp(-cum)
        v = v_ref[s]
        qkb_ref[s, pl.ds(base, TB_SCAN), :GLA_QKP] = q_dec
        qkb_ref[s, pl.ds(base, TB_SCAN), GLA_QKP:] = k_inv.astype(BF16)
        state = st_ref[s]
        yield
        for c in reversed(range(nch)):
            ob_ref[s, pl.ds(base + c * CHUNK, CHUNK), :] = _dot_nt(rows(q_dec, c), state)
            yield
            decay = jnp.exp(rows(cum, c)[0:1, :])
            contrib = _dot_tn(rows(v, c), rows(k_inv, c) * decay)
            yield
            state = state * decay + contrib * blockmask
            yield
        st_ref[s] = state

    def forward_sweep(s):
        base = pl.multiple_of(j * TB_SCAN, TB_SCAN)
        row_i = _iota((CHUNK, GLA_HEADS * CHUNK), 0)
        lane_j = _iota((CHUNK, GLA_HEADS * CHUNK), 1) & (CHUNK - 1)
        cum_f = decay_cum(s, 0)
        yield
        q_f = qk_ref[s, :, :GLA_QKP] * scale * jnp.exp(cum_f)
        yield
        k_f = qk_ref[s, :, GLA_QKP:] * jnp.exp(-cum_f)
        k_f16 = k_f.astype(BF16)
        yield
        v = v_ref[s]
        v16 = v.astype(BF16)
        q_b = qkb_ref[s, pl.ds(base, TB_SCAN), :GLA_QKP]
        k_b = qkb_ref[s, pl.ds(base, TB_SCAN), GLA_QKP:]
        state = st_ref[s]
        outs = []
        yield
        for c in range(nch):
            a_f = _dot_nt(rows(q_f, c), per_head_rows(rows(k_f16, c), kmask16))
            yield
            a_b = _dot_nt(rows(q_b, c), per_head_rows(rows(k_b, c), kmask16))
            yield
            scores = jnp.where(lane_j <= row_i, a_f, 0.0) + jnp.where(lane_j >= row_i, a_b, 0.0)
            o = _dot(scores, per_head_rows(rows(v16, c), vmask16))
            yield
            o = o + _dot_nt(rows(q_f, c), state)
            outs.append(o)
            yield
            decay = jnp.exp(rows(cum_f, c)[CHUNK - 1:CHUNK, :])
            contrib = _dot_tn(rows(v, c), rows(k_f, c) * decay)
            yield
            state = state * decay + contrib * blockmask
            yield
        st_ref[s] = state
        o = jnp.concatenate(outs, axis=0) + ob_ref[s, pl.ds(base, TB_SCAN), :]
        mean_sq = _dot_exact_rhs(o * o, sameh_ref[...], pieces=2) * (1.0 / GLA_DV)
        yield
        y_ref[s] = o * lax.rsqrt(mean_sq + EPS) * ng_ref[...] * _silu(r_ref[s])

    @pl.when(j == 0)
    def _():
        st_ref[...] = jnp.where(ph == 0, s0_ref[:, 1], s0_ref[:, 0])

    @pl.when(ph == 0)
    def _():
        _interleave(*(backward_sweep(s) for s in range(SCAN_BATCH)))

    @pl.when(ph == 1)
    def _():
        _interleave(*(forward_sweep(s) for s in range(SCAN_BATCH)))

    @pl.when(j == nblk - 1)
    def _():
        @pl.when(ph == 0)
        def _():
            sfin_ref[:, 1] = st_ref[...]

        @pl.when(ph == 1)
        def _():
            sfin_ref[:, 0] = st_ref[...]


def _gla(qk, v, r, glr, s0, wg, bg, ng, consts):
    batch, seq, _ = qk.shape
    nblk = seq // TB_SCAN
    sshape = (2, GLA_WIDTH, GLA_QKP)
    tri3, same_head2, gla_block = consts["tri3"], consts["same_head2"], consts["gla_block"]
    return pl.pallas_call(
        functools.partial(_gla_kernel, nblk=nblk),
        grid=(batch // SCAN_BATCH, 2, nblk),
        in_specs=[_seq_spec(P_QK, nblk), _seq_spec(GLA_WIDTH, nblk), _seq_phase1_spec(GLA_WIDTH, nblk),
                  _seq_spec(P_GLR, nblk), _state_spec(sshape),
                  _param_spec(wg.shape), _param_spec(bg.shape), _param_spec(ng.shape),
                  _param_spec(tri3.shape), _param_spec(same_head2.shape), _param_spec(gla_block.shape)],
        out_specs=[_seq_phase1_spec(GLA_WIDTH, nblk), _state_spec(sshape)],
        out_shape=[jax.ShapeDtypeStruct((batch, seq, GLA_WIDTH), F32),
                   jax.ShapeDtypeStruct((batch,) + sshape, F32)],
        scratch_shapes=[pltpu.VMEM((SCAN_BATCH, GLA_WIDTH, GLA_QKP), F32),
                        pltpu.VMEM((SCAN_BATCH, seq, GLA_WIDTH), F32),
                        pltpu.VMEM((SCAN_BATCH, seq, P_QK), BF16)],
        compiler_params=_scan_params(),
        name="gla",
    )(qk, v, r, glr, s0, wg, bg, ng, tri3, same_head2, gla_block)


def _ssd_kernel(z_ref, x_ref, xp_ref, xn_ref, dt_ref, s0_ref, cw_ref, cb_ref, dtb_ref, alog_ref, dsk_ref, ng_ref,
                tri_ref, exp_ref, y_ref, sfin_ref, st_ref, yb_ref, ext_ref, xs_ref, bc_ref, *, nblk):
    ph = pl.program_id(1)
    j = pl.program_id(2)
    blk = _scan_blk(ph, j, nblk)
    base = pl.multiple_of(blk * TB_SCAN, TB_SCAN)
    nch = TB_SCAN // CHUNK

    lane_h = _block_id((1, SSD_WIDTH), 1, SSD_HEADDIM)
    lane_g = _block_id((1, SSD_WIDTH), 1, SSD_HEADDIM * SSD_REP)
    lane_n = _block_id((1, SSD_BC), 1, SSD_STATE)
    head_lanes = [(lane_h == h).astype(BF16) for h in range(SSD_HEADS)]
    group_lanes = [(lane_g == g).astype(BF16) for g in range(SSD_GROUPS)]
    group_states = [(lane_n == g).astype(BF16) for g in range(SSD_GROUPS)]

    def rows(x, c):
        return x[c * CHUNK:(c + 1) * CHUNK]

    def conv_silu(s):
        ext_ref[s, 0:HALO, :] = jnp.where(blk > 0, xp_ref[s], 0.0)
        ext_ref[s, HALO:HALO + TB_SCAN, :] = x_ref[s]
        ext_ref[s, HALO + TB_SCAN:, :] = jnp.where(blk < nblk - 1, xn_ref[s], 0.0)
        pad = (SSD_CONV - 1) // 2
        acc = cb_ref[...]
        for t in range(SSD_CONV):
            acc = acc + ext_ref[s, pl.ds(HALO - pad + t, TB_SCAN), :] * cw_ref[t:t + 1, :]
        return _silu(acc)

    def decay_inputs(s):
        dt_c = jax.nn.softplus(dt_ref[s] + dtb_ref[...])
        return dt_c, dt_c * -jnp.exp(alog_ref[...])

    def expanded(d, cum_c, dt_c):
        both = _dot_exact_rhs(jnp.concatenate([cum_c, dt_c], axis=0), exp_ref[:, d * SSD_WIDTH:(d + 1) * SSD_WIDTH])
        return both[:TB_SCAN], both[TB_SCAN:]

    def expanded_both(cum_cf, cum_cb, dt_c):
        cum_c = jnp.where(_iota((TB_SCAN, LANE), 1) < SSD_HEADS, cum_cf, cum_cb)
        both = _dot_exact_rhs(jnp.concatenate([cum_c, dt_c], axis=0), exp_ref[...])
        return (both[:TB_SCAN, :SSD_WIDTH], both[TB_SCAN:, :SSD_WIDTH],
                both[:TB_SCAN, SSD_WIDTH:], both[TB_SCAN:, SSD_WIDTH:])

    def scan_terms(d, cum_e, dt_e):
        cum = cum_e.reshape(nch, CHUNK, SSD_WIDTH)
        last = cum[:, CHUNK - 1:CHUNK, :] if d == 0 else cum[:, 0:1, :]
        weight = jnp.exp(last - cum).reshape(TB_SCAN, SSD_WIDTH) * dt_e
        return jnp.exp(cum_e), weight, jnp.exp(last)

    def advance(state, decay, bm_c, xw_c):
        xw16 = xw_c.astype(BF16)
        contrib = [_dot_tn(bm_c[:, g * SSD_STATE:(g + 1) * SSD_STATE], xw16 * group_lanes[g])
                   for g in range(SSD_GROUPS)]
        return state * decay + jnp.concatenate(contrib, axis=0)

    def backward_sweep(s):
        xbc = conv_silu(s)
        yield
        dt_c, la_c = decay_inputs(s)
        cum_c = _dot_exact_lhs(tri_ref[1], la_c)
        yield
        xs = xbc[:, :SSD_WIDTH]
        bc16 = xbc[:, SSD_WIDTH:].astype(BF16)
        xs_ref[s, pl.ds(base, TB_SCAN), :] = xs
        bc_ref[s, pl.ds(base, TB_SCAN), :] = bc16
        bm, cm = bc16[:, :SSD_BC], bc16[:, SSD_BC:]
        cum_e, dt_e = expanded(1, cum_c, dt_c)
        yield
        grow, weight, decay = scan_terms(1, cum_e, dt_e)
        xw = xs * weight
        state = st_ref[s]
        outs = [None] * nch
        yield
        for c in reversed(range(nch)):
            outs[c] = _dot(rows(cm, c), state)
            state = advance(state, decay[c], rows(bm, c), rows(xw, c))
            yield
        st_ref[s] = state
        yb_ref[s, pl.ds(base, TB_SCAN), :] = jnp.concatenate(outs, axis=0) * grow

    def forward_sweep(s):
        xs = xs_ref[s, pl.ds(base, TB_SCAN), :]
        bm = bc_ref[s, pl.ds(base, TB_SCAN), :SSD_BC]
        cm = bc_ref[s, pl.ds(base, TB_SCAN), SSD_BC:]
        dt_c, la_c = decay_inputs(s)
        pieces = jnp.concatenate(_split3(la_c), axis=0)
        cum_cf = jnp.dot(tri_ref[0], pieces, preferred_element_type=F32)
        cum_cb = jnp.dot(tri_ref[1], pieces, preferred_element_type=F32)
        yield
        cum_f, dt_f, cum_b, dt_b = expanded_both(cum_cf, cum_cb, dt_c)
        yield
        grow, weight, decay = scan_terms(0, cum_f, dt_f)
        xw = xs * weight
        ci = _iota((CHUNK, SSD_WIDTH), 0)
        cj = _iota((CHUNK, SSD_WIDTH), 1) & (CHUNK - 1)
        diag = (_iota((TB_SCAN, SSD_WIDTH), 1) & (CHUNK - 1)) == (_iota((TB_SCAN, SSD_WIDTH), 0) & (CHUNK - 1))

        def at_j(x):
            return jnp.sum(jnp.where(diag, x, 0.0).reshape(nch, CHUNK, SSD_WIDTH), axis=1, keepdims=True)

        cum_row_f, cum_row_b, dt_row_f, dt_row_b = at_j(cum_f), at_j(cum_b), at_j(dt_f), at_j(dt_b)
        xs16 = xs.astype(BF16)
        state = st_ref[s]
        outs = []
        yield
        for c in range(nch):
            bm_group = [rows(bm, c) * group_states[g] for g in range(SSD_GROUPS)]
            bm_heads = jnp.concatenate([bm_group[h // SSD_REP] for h in range(SSD_HEADS)], axis=0)
            scores = _dot_nt(rows(cm, c), bm_heads)
            yield
            l_f = jnp.exp(jnp.where(cj <= ci, rows(cum_f, c) - cum_row_f[c], -jnp.inf))
            l_b = jnp.exp(jnp.where(cj >= ci, rows(cum_b, c) - cum_row_b[c], -jnp.inf))
            mat = scores * (l_f * dt_row_f[c] + l_b * dt_row_b[c])
            x_heads = jnp.concatenate([rows(xs16, c) * head_lanes[h] for h in range(SSD_HEADS)], axis=0)
            outs.append(_dot(mat, x_heads) + _dot(rows(cm, c), state) * rows(grow, c))
            yield
            state = advance(state, decay[c], rows(bm, c), rows(xw, c))
            yield
        st_ref[s] = state
        y = jnp.concatenate(outs, axis=0) + yb_ref[s, pl.ds(base, TB_SCAN), :] + xs * dsk_ref[...]
        y_ref[s] = _rms(y * _silu(z_ref[s]), ng_ref[...])

    @pl.when(j == 0)
    def _():
        st_ref[...] = jnp.where(ph == 0, s0_ref[:, 1], s0_ref[:, 0])

    @pl.when(ph == 0)
    def _():
        _interleave(*(backward_sweep(s) for s in range(SCAN_BATCH)))

    @pl.when(ph == 1)
    def _():
        _interleave(*(forward_sweep(s) for s in range(SCAN_BATCH)))

    @pl.when(j == nblk - 1)
    def _():
        @pl.when(ph == 0)
        def _():
            sfin_ref[:, 1] = st_ref[...]

        @pl.when(ph == 1)
        def _():
            sfin_ref[:, 0] = st_ref[...]


def _ssd(z, xbc, dt, s0, cw, cb, dtb, alog, dsk, ng, consts):
    batch, seq, _ = z.shape
    nblk = seq // TB_SCAN
    tiles = TB_SCAN // HALO
    sshape = (2, SSD_BC, SSD_WIDTH)

    def prev_map(g, ph, j):
        return (g, jnp.maximum(_phase0_blk(ph, j, nblk) * tiles - 1, 0), 0)

    def next_map(g, ph, j):
        return (g, jnp.minimum((_phase0_blk(ph, j, nblk) + 1) * tiles, seq // HALO - 1), 0)

    halo = (SCAN_BATCH, HALO, SSD_XBC)
    tri3, expand3 = consts["tri3"], consts["expand3"]
    return pl.pallas_call(
        functools.partial(_ssd_kernel, nblk=nblk),
        grid=(batch // SCAN_BATCH, 2, nblk),
        in_specs=[_seq_phase1_spec(SSD_WIDTH, nblk), _seq_phase0_spec(SSD_XBC, nblk),
                  pl.BlockSpec(halo, prev_map), pl.BlockSpec(halo, next_map),
                  _seq_spec(P_DT, nblk), _state_spec(sshape),
                  _param_spec(cw.shape), _param_spec(cb.shape), _param_spec(dtb.shape), _param_spec(alog.shape),
                  _param_spec(dsk.shape), _param_spec(ng.shape), _param_spec(tri3.shape), _param_spec(expand3.shape)],
        out_specs=[_seq_phase1_spec(SSD_WIDTH, nblk), _state_spec(sshape)],
        out_shape=[jax.ShapeDtypeStruct((batch, seq, SSD_WIDTH), F32),
                   jax.ShapeDtypeStruct((batch,) + sshape, F32)],
        scratch_shapes=[pltpu.VMEM((SCAN_BATCH, SSD_BC, SSD_WIDTH), F32),
                        pltpu.VMEM((SCAN_BATCH, seq, SSD_WIDTH), F32),
                        pltpu.VMEM((SCAN_BATCH, TB_SCAN + 2 * HALO, SSD_XBC), F32),
                        pltpu.VMEM((SCAN_BATCH, seq, SSD_WIDTH), F32),
                        pltpu.VMEM((SCAN_BATCH, seq, 2 * SSD_BC), BF16)],
        compiler_params=_scan_params(),
        name="ssd",
    )(z, xbc, xbc, xbc, dt, s0, cw, cb, dtb, alog, dsk, ng, tri3, expand3)


def _s5_kernel(*refs, reverse, final, batch):
    if final:
        (u_ref, x0_ref, bblk_ref, a_ref, cblk_ref, yprev_ref, d_ref, wglu_ref, bglu_ref,
         y_ref, xfin_ref, xs_ref, st_ref) = refs
    else:
        u_ref, x0_ref, bblk_ref, a_ref, cblk_ref, y_ref, xfin_ref, xs_ref, st_ref = refs
    i = pl.program_id(0)
    rows = TT_S5 * batch

    @pl.when(i == 0)
    def _():
        st_ref[...] = x0_ref[...]

    u = u_ref[...].reshape(rows, S5_WIDTH)
    u16 = u.astype(BF16)
    piece = 2 * S5_HALF // S5_PARTS
    readout = [None] * S5_STREAMS

    def stream(h):
        lo = h * 2 * S5_HALF
        re = slice(lo, lo + S5_HALF)
        im = slice(lo + S5_HALF, lo + 2 * S5_HALF)
        for p in range(S5_PARTS):
            cols = slice(lo + p * piece, lo + (p + 1) * piece)
            xs_ref[:, cols] = jnp.dot(u16, bblk_ref[:, cols], preferred_element_type=F32)
            yield
        a_re = jnp.broadcast_to(a_ref[0:1, h * S5_HALF:(h + 1) * S5_HALF], (batch, S5_HALF))
        a_im = jnp.broadcast_to(a_ref[1:2, h * S5_HALF:(h + 1) * S5_HALF], (batch, S5_HALF))
        x_re = st_ref[:, re]
        x_im = st_ref[:, im]
        for n in range(TT_S5):
            t = (TT_S5 - 1 - n) if reverse else n
            row = slice(t * batch, (t + 1) * batch)
            x_re, x_im = (a_re * x_re - a_im * x_im + xs_ref[row, re],
                          a_re * x_im + a_im * x_re + xs_ref[row, im])
            xs_ref[row, re] = x_re
            xs_ref[row, im] = x_im
            if (n + 1) % (TT_S5 // S5_PARTS) == 0:
                yield
        st_ref[:, re] = x_re
        st_ref[:, im] = x_im
        xfin_ref[:, re] = x_re
        xfin_ref[:, im] = x_im
        acc = None
        for p in range(S5_PARTS):
            cols = slice(lo + p * piece, lo + (p + 1) * piece)
            part = jnp.dot(xs_ref[:, cols].astype(BF16), cblk_ref[cols, :], preferred_element_type=F32)
            acc = part if acc is None else acc + part
            yield
        readout[h] = acc

    _interleave(*(_lagged(stream(h), h * S5_PARTS) for h in range(S5_STREAMS)))
    y = sum(readout[1:], readout[0])
    if final:
        y = y + yprev_ref[...].reshape(rows, S5_WIDTH) + u * d_ref[...]
        g = jax.nn.gelu(y)
        y = g * jax.nn.sigmoid(_dot(g, wglu_ref[...]) + bglu_ref[...])
    y_ref[...] = y.reshape(TT_S5, batch, S5_WIDTH)


def _s5_dir(u, x0, bblk, a, cblk, extra, reverse):
    seq, batch, _ = u.shape
    nblk = seq // TT_S5
    final = extra is not None

    def seq_map(i):
        return ((nblk - 1 - i) if reverse else i, 0, 0)

    def const(shape):
        nd = len(shape)
        return pl.BlockSpec(shape, lambda i: (0,) * nd)

    seq_spec = pl.BlockSpec((TT_S5, batch, S5_WIDTH), seq_map)
    in_specs = [seq_spec, const(x0.shape), const(bblk.shape), const(a.shape), const(cblk.shape)]
    args = [u, x0, bblk, a, cblk]
    if final:
        yprev, d, wglu, bglu = extra
        in_specs += [seq_spec, const(d.shape), const(wglu.shape), const(bglu.shape)]
        args += [yprev, d, wglu, bglu]
    return pl.pallas_call(
        functools.partial(_s5_kernel, reverse=reverse, final=final, batch=batch),
        grid=(nblk,),
        in_specs=in_specs,
        out_specs=[seq_spec, const(x0.shape)],
        out_shape=[jax.ShapeDtypeStruct(u.shape, F32), jax.ShapeDtypeStruct(x0.shape, F32)],
        scratch_shapes=[pltpu.VMEM((TT_S5 * batch, 2 * S5_LANES), F32), pltpu.VMEM(x0.shape, F32)],
        compiler_params=pltpu.CompilerParams(dimension_semantics=("arbitrary",), vmem_limit_bytes=VMEM_LIMIT),
        name="s5_bwd" if reverse else "s5_fwd",
    )(*args)


def _s5(u_tm, x0, prm):
    y_b, xf_b = _s5_dir(u_tm, x0[1], prm["bblk"][1], prm["a"][1], prm["cblk"], None, True)
    y, xf_f = _s5_dir(u_tm, x0[0], prm["bblk"][0], prm["a"][0], prm["cblk"],
                      (y_b, prm["d"], prm["wglu"], prm["bglu"]), False)
    return y, jnp.stack([xf_f, xf_b])


def _s5_zoh_kernel(are_ref, aim_ref, ldt_ref, bre_ref, bim_ref, abre_ref, abim_ref, bbre_ref, bbim_ref):
    a_re = are_ref[...]
    a_im = aim_ref[...]
    dt = jnp.exp(ldt_ref[...])
    mag = jnp.exp(dt * a_re)
    ab_re = mag * jnp.cos(dt * a_im)
    ab_im = mag * jnp.sin(dt * a_im)
    den = a_re * a_re + a_im * a_im
    num_re = ab_re - 1.0
    num_im = ab_im
    f_re = (num_re * a_re + num_im * a_im) / den
    f_im = (num_im * a_re - num_re * a_im) / den
    abre_ref[...] = ab_re
    abim_ref[...] = ab_im
    b_re = bre_ref[...]
    b_im = bim_ref[...]
    for d in range(2):
        bbre_ref[d] = f_re[d:d + 1] * b_re - f_im[d:d + 1] * b_im
        bbim_ref[d] = f_re[d:d + 1] * b_im + f_im[d:d + 1] * b_re


def _s5_params(a_re, a_im, log_dt, b_re, b_im, c_re, c_im, d, w_glu, b_glu):
    lanes = lambda t: t.reshape(2, S5_LANES)
    b_lanes = lambda b: b.transpose(2, 0, 1).reshape(S5_GROUP, S5_LANES)
    vec = jax.ShapeDtypeStruct((2, S5_LANES), F32)
    mat = jax.ShapeDtypeStruct((2, S5_GROUP, S5_LANES), F32)
    ab_re, ab_im, bb_re, bb_im = pl.pallas_call(
        _s5_zoh_kernel, out_shape=[vec, vec, mat, mat], name="s5_zoh",
    )(lanes(a_re), lanes(a_im), lanes(jnp.repeat(log_dt, S5_STATE, axis=-1)), b_lanes(b_re), b_lanes(b_im))
    eye = jnp.eye(S5_GROUPS, dtype=F32)

    def in_block(bb):
        bb = bb.reshape(2, S5_GROUP, S5_GROUPS, S5_STATE)
        return jnp.einsum("dhgp,gk->dghkp", bb, eye).reshape(2, S5_WIDTH, S5_LANES)

    def out_block(c):
        return jnp.einsum("ghp,gk->gpkh", c, eye).reshape(S5_LANES, S5_WIDTH)

    def stream_major(re, im, axis):
        parts = []
        for h in range(S5_STREAMS):
            idx = [slice(None)] * re.ndim
            idx[axis] = slice(h * S5_HALF, (h + 1) * S5_HALF)
            parts += [re[tuple(idx)], im[tuple(idx)]]
        return jnp.concatenate(parts, axis=axis)

    return {
        "a": jnp.stack([ab_re.reshape(2, S5_LANES), ab_im.reshape(2, S5_LANES)], axis=1),
        "bblk": stream_major(in_block(bb_re), in_block(bb_im), 2).astype(BF16),
        "cblk": stream_major(out_block(c_re), -out_block(c_im), 0).astype(BF16),
        "d": d.reshape(1, S5_WIDTH),
        "wglu": w_glu.astype(BF16),
        "bglu": b_glu.reshape(1, S5_WIDTH),
    }


def _pad_heads(w, heads, width, padded):
    lead = w.shape[:-1]
    w = w.reshape(lead + (heads, width))
    w = jnp.pad(w, [(0, 0)] * len(lead) + [(0, 0), (0, padded - width)])
    return w.reshape(lead + (heads * padded,))


def _pad_last(w, width):
    return jnp.pad(w, [(0, 0)] * (w.ndim - 1) + [(0, width - w.shape[-1])])


def _pack_w_in(w_in):
    pts = []
    acc = 0
    for s in IN_SPLITS:
        pts.append((acc, acc + s))
        acc += s
    q, k, v, r, glr, s5, z, xbc, dt = (w_in[..., a:b] for a, b in pts)
    return jnp.concatenate([
        _pad_heads(q, GLA_HEADS, GLA_DK, GLA_DKP), _pad_heads(k, GLA_HEADS, GLA_DK, GLA_DKP), v, r,
        _pad_last(glr, P_GLR), s5, z, xbc, _pad_last(dt, P_DT)], axis=-1).astype(BF16)


def _pack_gla_gate(w_gate, b_gate):
    w = _pad_heads(w_gate, GLA_HEADS, GLA_DK, GLA_DKP)
    wp = jnp.zeros((2, P_GLR, GLA_QKP), F32)
    for d in range(2):
        wp = wp.at[d, d * GLA_RANK:(d + 1) * GLA_RANK].set(w[d])
    return wp.astype(BF16), _pad_heads(b_gate, GLA_HEADS, GLA_DK, GLA_DKP).reshape(2, 1, GLA_QKP)


def _per_seq(t, batch):
    n, ch = t.shape
    return t.reshape(batch, n // batch, ch)


def _to_col_major(t):
    batch, n, ch = t.shape
    return t.reshape(batch, n // GRID_W, GRID_W, ch).transpose(0, 2, 1, 3).reshape(batch, n, ch)


def _from_col_major(t):
    batch, n, ch = t.shape
    return t.reshape(batch, GRID_W, n // GRID_W, ch).transpose(0, 2, 1, 3).reshape(batch, n, ch)


def _to_time_major(t, batch):
    n, ch = t.shape
    return t.reshape(batch, n // batch, ch).transpose(1, 0, 2)


def _from_time_major(t):
    seq, batch, ch = t.shape
    return t.transpose(1, 0, 2).reshape(seq * batch, ch)


def kernel(x, c, ctx, c_ctx, ada_w, ada_b, norm_g, w_in, w_out, ff_w_gate, ff_w_up, ff_w_down, gla_w_gate, gla_b_gate, gla_norm_g, s5_a_re, s5_a_im, s5_log_dt, s5_b_re, s5_b_im, s5_c_re, s5_c_im, s5_d, s5_w_glu, s5_b_glu, ssd_conv_w, ssd_conv_b, ssd_dt_bias, ssd_a_log, ssd_d, ssd_norm_g, final_norm_g):
    batch, seq, _ = x.shape
    ctx_len = ctx.shape[1]
    depth = ada_w.shape[0]
    assert seq == GRID_W * CHUNK and ctx_len % TB_SCAN == 0 and seq % TM_TOKENS == 0
    assert batch == SUBLANE and (batch * ctx_len) % TM_TOKENS == 0 and batch % SCAN_BATCH == 0

    mod_rows = 2 * SUBLANE
    cc = jnp.zeros((mod_rows, D_MODEL), F32).at[:batch].set(c).at[batch].set(c_ctx)
    mods = _ada(cc, ada_w, ada_b).reshape(depth, mod_rows, N_MOD, D_MODEL)
    lat_mod = _mod_spec(seq // TM_TOKENS, None)
    ctx_mod = _mod_spec(None, batch)

    h = x.reshape(batch * seq, D_MODEL)
    hc = ctx.reshape(batch * ctx_len, D_MODEL)
    fg = final_norm_g.reshape(1, D_MODEL)
    wg, wu, wd = ff_w_gate.astype(BF16), ff_w_up.astype(BF16), ff_w_down.astype(BF16)
    w_all = _pack_w_in(w_in)
    wo = w_out.astype(BF16)
    consts = _scan_consts()

    for i in range(depth):
        ctx_out = i < depth - 1
        mod = mods[i]
        ng = norm_g[i]
        gla_wg, gla_bg = _pack_gla_gate(gla_w_gate[i], gla_b_gate[i])
        gla_ng = gla_norm_g[i].reshape(1, GLA_WIDTH)
        s5p = _s5_params(s5_a_re[i], s5_a_im[i], s5_log_dt[i], s5_b_re[i], s5_b_im[i], s5_c_re[i], s5_c_im[i],
                         s5_d[i], s5_w_glu[i], s5_b_glu[i])
        cw = jnp.pad(ssd_conv_w[i], ((0, SUBLANE - SSD_CONV), (0, 0)))
        cb = ssd_conv_b[i].reshape(1, SSD_XBC)
        dtb = _pad_last(ssd_dt_bias[i].reshape(1, 2 * SSD_HEADS), P_DT)
        alog = _pad_last(ssd_a_log[i].reshape(1, 2 * SSD_HEADS), P_DT)
        dsk = jnp.repeat(ssd_d[i], SSD_HEADDIM).reshape(1, SSD_WIDTH)
        ssd_ng = ssd_norm_g[i].reshape(1, SSD_WIDTH)

        h = _ffn1(h, mod, lat_mod, ng, wg, wu, wd, i)
        hc = _ffn1(hc, mod, ctx_mod, ng, wg, wu, wd, i)

        qk_l, v_l, r_l, glr_l, s5_l, z_l, xbc_l, dt_l = _inproj(h, mod, lat_mod, ng, w_all, i)
        qk_c, v_c, r_c, glr_c, s5_c, z_c, xbc_c, dt_c = _inproj(hc, mod, ctx_mod, ng, w_all, i)

        gla_s0 = jnp.zeros((batch, 2, GLA_WIDTH, GLA_QKP), F32)
        seqs = functools.partial(_per_seq, batch=batch)
        yg_c, gla_s = _gla(seqs(qk_c), seqs(v_c), seqs(r_c), seqs(glr_c), gla_s0, gla_wg, gla_bg, gla_ng, consts)
        yg_l, _ = _gla(seqs(qk_l), seqs(v_l), seqs(r_l), seqs(glr_l), gla_s, gla_wg, gla_bg, gla_ng, consts)
        yg_c = yg_c.reshape(-1, GLA_WIDTH)
        yg_l = yg_l.reshape(-1, GLA_WIDTH)

        s5_x0 = jnp.zeros((2, batch, 2 * S5_LANES), F32)
        ys_c, s5_x = _s5(_to_time_major(s5_c, batch), s5_x0, s5p)
        ys_l, _ = _s5(_to_time_major(s5_l, batch), s5_x, s5p)
        ys_l = _from_time_major(ys_l)

        ssd_s0 = jnp.zeros((batch, 2, SSD_BC, SSD_WIDTH), F32)
        yc_c, ssd_s = _ssd(seqs(z_c), seqs(xbc_c), seqs(dt_c), ssd_s0, cw, cb, dtb, alog, dsk, ssd_ng, consts)
        yc_l, _ = _ssd(_to_col_major(seqs(z_l)), _to_col_major(seqs(xbc_l)), _to_col_major(seqs(dt_l)),
                       ssd_s, cw, cb, dtb, alog, dsk, ssd_ng, consts)
        yc_c = yc_c.reshape(-1, SSD_WIDTH)
        yc_l = _from_col_major(yc_l).reshape(-1, SSD_WIDTH)

        h = _mix_ffn2(h, yg_l, ys_l, yc_l, mod, lat_mod, ng, wo, wg, wu, wd, fg, i, final=not ctx_out)
        if ctx_out:
            hc = _mix_ffn2(hc, yg_c, _from_time_major(ys_c), yc_c, mod, ctx_mod, ng, wo, wg, wu, wd, fg, i,
                           final=False)
    return h.reshape(batch, seq, D_MODEL)
```

```python
import functools
import math

import jax
import jax.numpy as jnp
from jax import lax
from jax.experimental import pallas as pl
from jax.experimental.pallas import tpu as pltpu

F32 = jnp.float32
BF16 = jnp.bfloat16
HI = lax.Precision.HIGHEST

D_MODEL = 1024
GRID_W = 64
N_MOD = 9
D_FF = 2816
EPS = 1e-6

GLA_HEADS = 4
GLA_DK = 48
GLA_DKP = 64
GLA_DV = 96
GLA_RANK = 16
GLA_TAU = 16.0
GLA_QK = GLA_HEADS * GLA_DK
GLA_QKP = GLA_HEADS * GLA_DKP
GLA_WIDTH = GLA_HEADS * GLA_DV

S5_WIDTH = 256
S5_GROUP = 16
S5_GROUPS = 16
S5_STATE = 64
S5_LANES = S5_GROUPS * S5_STATE
S5_STREAMS = 4
S5_HALF = S5_LANES // S5_STREAMS
S5_PARTS = 2

SSD_HEADS = 6
SSD_HEADDIM = 64
SSD_GROUPS = 2
SSD_REP = SSD_HEADS // SSD_GROUPS
SSD_STATE = 128
SSD_CONV = 5
SSD_WIDTH = SSD_HEADS * SSD_HEADDIM
SSD_BC = SSD_GROUPS * SSD_STATE
SSD_XBC = SSD_WIDTH + 2 * SSD_BC

MIX_WIDTH = GLA_WIDTH + S5_WIDTH + SSD_WIDTH
IN_SPLITS = (GLA_QK, GLA_QK, GLA_WIDTH, GLA_WIDTH, 2 * GLA_RANK, S5_WIDTH, SSD_WIDTH, SSD_XBC, 2 * SSD_HEADS)

CHUNK = 64
LANE = 128
SUBLANE = 8
HALO = SUBLANE

P_QK = 2 * GLA_QKP
P_GLR = LANE
P_DT = LANE
P_TOTAL = P_QK + 2 * GLA_WIDTH + P_GLR + S5_WIDTH + SSD_WIDTH + SSD_XBC + P_DT

TM_TOKENS = 512
TB_SCAN = 256
SCAN_BATCH = 2
TT_S5 = 256
VMEM_LIMIT = 56 * 1024 * 1024


def _silu(x):
    half = 0.5 * x
    return half + half * jnp.tanh(half)


def _rms(x, g):
    return x * lax.rsqrt(jnp.mean(x * x, axis=-1, keepdims=True) + EPS) * g


def _dot(a, b):
    return jnp.dot(a.astype(BF16), b.astype(BF16), preferred_element_type=F32)


def _dot_nt(a, b):
    return lax.dot_general(a.astype(BF16), b.astype(BF16), (((1,), (1,)), ((), ())), preferred_element_type=F32)


def _dot_tn(a, b):
    return lax.dot_general(a.astype(BF16), b.astype(BF16), (((0,), (0,)), ((), ())), preferred_element_type=F32)


def _dot_hi(a, b):
    return jnp.dot(a, b, precision=HI, preferred_element_type=F32)


def _split3(x, pieces=3):
    out = []
    rest = x
    for _ in range(pieces):
        part = rest.astype(BF16)
        out.append(part)
        rest = rest - part.astype(F32)
    return tuple(out)


def _dot_exact_rhs(x, m3, pieces=3):
    return jnp.dot(jnp.concatenate(_split3(x, pieces), axis=1), m3, preferred_element_type=F32)


def _dot_exact_lhs(m3, x):
    return jnp.dot(m3, jnp.concatenate(_split3(x), axis=0), preferred_element_type=F32)


_DONE = object()


def _interleave(*streams):
    live = list(streams)
    while live:
        for stream in tuple(live):
            if next(stream, _DONE) is _DONE:
                live.remove(stream)


def _lagged(stream, stages):
    for _ in range(stages):
        yield
    yield from stream


def _iota(shape, dim):
    return lax.broadcasted_iota(jnp.int32, shape, dim)


def _block_id(shape, dim, width):
    idx = _iota(shape, dim)
    out = jnp.zeros(shape, F32)
    for k in range(1, -(-shape[dim] // width)):
        out = out + jnp.where(idx >= k * width, 1.0, 0.0)
    return out


def _chunk_tri(n, lower):
    i = _iota((n, n), 0)
    j = _iota((n, n), 1)
    same_chunk = (i & -CHUNK) == (j & -CHUNK)
    return jnp.where(same_chunk & ((j <= i) if lower else (j >= i)), 1.0, 0.0).astype(BF16)


def _ada_kernel(c_ref, w_ref, b_ref, o_ref):
    o_ref[0] = _dot_hi(_silu(c_ref[...]), w_ref[0]) + b_ref[0]


def _ada(cc, ada_w, ada_b):
    depth = ada_w.shape[0]
    rows = cc.shape[0]
    tn = D_MODEL
    return pl.pallas_call(
        _ada_kernel,
        grid=(depth, N_MOD * D_MODEL // tn),
        in_specs=[pl.BlockSpec((rows, D_MODEL), lambda l, n: (0, 0)),
                  pl.BlockSpec((1, D_MODEL, tn), lambda l, n: (l, 0, n)),
                  pl.BlockSpec((1, 1, tn), lambda l, n: (l, 0, n))],
        out_specs=pl.BlockSpec((1, rows, tn), lambda l, n: (l, 0, n)),
        out_shape=jax.ShapeDtypeStruct((depth, rows, N_MOD * D_MODEL), F32),
        name="ada_mod",
    )(cc, ada_w, ada_b.reshape(depth, 1, N_MOD * D_MODEL))


def _ffn_body(x, m, g, wg_ref, wu_ref, wd_ref, base):
    u = (_rms(x, g) * (1.0 + m[base + 1:base + 2]) + m[base:base + 1]).astype(BF16)
    gate = jnp.dot(u, wg_ref[...], preferred_element_type=F32)
    up = jnp.dot(u, wu_ref[...], preferred_element_type=F32)
    act = (_silu(gate) * up).astype(BF16)
    y = jnp.dot(act, wd_ref[...], preferred_element_type=F32)
    return x + 0.5 * m[base + 2:base + 3] * y


def _ffn1_kernel(h_ref, mod_ref, ng_ref, wg_ref, wu_ref, wd_ref, o_ref):
    o_ref[...] = _ffn_body(h_ref[...], mod_ref[0], ng_ref[0:1], wg_ref, wu_ref, wd_ref, 0)


def _mix_ffn2_kernel(h_ref, yg_ref, ys_ref, yc_ref, mod_ref, ng_ref, wo_ref,
                     wg_ref, wu_ref, wd_ref, fg_ref, o_ref, *, final):
    m = mod_ref[0]
    mixed = jnp.concatenate([yg_ref[...], ys_ref[...], yc_ref[...]], axis=1).astype(BF16)
    mix = jnp.dot(mixed, wo_ref[...], preferred_element_type=F32)
    x = h_ref[...] + m[5:6] * mix
    x = _ffn_body(x, m, ng_ref[2:3], wg_ref, wu_ref, wd_ref, 6)
    if final:
        x = _rms(x, fg_ref[...])
    o_ref[...] = x


def _const_spec(shape):
    nd = len(shape)
    return pl.BlockSpec(shape, lambda t: (0,) * nd, pipeline_mode=pl.Buffered(1))


def _layer_spec(stacked, *lead):
    tail = stacked.shape[len(lead):]
    return pl.BlockSpec((None,) * len(lead) + tail, lambda t: lead + (0,) * len(tail),
                        pipeline_mode=pl.Buffered(1))


def _tile_spec(width):
    return pl.BlockSpec((TM_TOKENS, width), lambda t: (t, 0))


def _mod_spec(tiles_per_row, fixed_row):
    if fixed_row is None:
        return pl.BlockSpec((1, N_MOD, D_MODEL), lambda t: (t // tiles_per_row, 0, 0))
    return pl.BlockSpec((1, N_MOD, D_MODEL), lambda t: (fixed_row, 0, 0))


def _dense_params():
    return pltpu.CompilerParams(dimension_semantics=("arbitrary",), vmem_limit_bytes=VMEM_LIMIT)


def _ffn1(h, mod, mod_spec, ng, wg, wu, wd, layer):
    n = h.shape[0]
    return pl.pallas_call(
        _ffn1_kernel,
        grid=(n // TM_TOKENS,),
        in_specs=[_tile_spec(D_MODEL), mod_spec, _const_spec(ng.shape),
                  _layer_spec(wg, layer, 0), _layer_spec(wu, layer, 0), _layer_spec(wd, layer, 0)],
        out_specs=_tile_spec(D_MODEL),
        out_shape=jax.ShapeDtypeStruct(h.shape, F32),
        compiler_params=_dense_params(),
        name="ffn1",
    )(h, mod, ng, wg, wu, wd)


def _mix_ffn2(h, yg, ys, yc, mod, mod_spec, ng, wo, wg, wu, wd, fg, layer, final):
    n = h.shape[0]
    return pl.pallas_call(
        functools.partial(_mix_ffn2_kernel, final=final),
        grid=(n // TM_TOKENS,),
        in_specs=[_tile_spec(D_MODEL), _tile_spec(GLA_WIDTH), _tile_spec(S5_WIDTH), _tile_spec(SSD_WIDTH),
                  mod_spec, _const_spec(ng.shape), _layer_spec(wo, layer),
                  _layer_spec(wg, layer, 1), _layer_spec(wu, layer, 1), _layer_spec(wd, layer, 1),
                  _const_spec(fg.shape)],
        out_specs=_tile_spec(D_MODEL),
        out_shape=jax.ShapeDtypeStruct(h.shape, F32),
        compiler_params=_dense_params(),
        name="mix_ffn2",
    )(h, yg, ys, yc, mod, ng, wo, wg, wu, wd, fg)


_P_WIDTHS = (P_QK, GLA_WIDTH, GLA_WIDTH, P_GLR, S5_WIDTH, SSD_WIDTH, SSD_XBC, P_DT)


def _inproj_kernel(h_ref, mod_ref, ng_ref, w_ref, *out_refs):
    m = mod_ref[0]
    u = _rms(h_ref[...], ng_ref[1:2]) * (1.0 + m[4:5]) + m[3:4]
    p = jnp.dot(u.astype(BF16), w_ref[...], preferred_element_type=F32)
    off = 0
    for ref, width in zip(out_refs, _P_WIDTHS):
        ref[...] = p[:, off:off + width]
        off += width


def _inproj(h, mod, mod_spec, ng, w_all, layer):
    n = h.shape[0]
    return pl.pallas_call(
        _inproj_kernel,
        grid=(n // TM_TOKENS,),
        in_specs=[_tile_spec(D_MODEL), mod_spec, _const_spec(ng.shape), _layer_spec(w_all, layer)],
        out_specs=[_tile_spec(w) for w in _P_WIDTHS],
        out_shape=[jax.ShapeDtypeStruct((n, w), F32) for w in _P_WIDTHS],
        compiler_params=_dense_params(),
        name="inproj",
    )(h, mod, ng, w_all)


def _scan_blk(ph, j, nblk):
    return jnp.where(ph == 0, nblk - 1 - j, j)


def _seq_spec(width, nblk):
    return pl.BlockSpec((SCAN_BATCH, TB_SCAN, width), lambda g, ph, j: (g, _scan_blk(ph, j, nblk), 0))


def _phase0_blk(ph, j, nblk):
    return jnp.where(ph == 0, nblk - 1 - j, 0)


def _seq_phase0_spec(width, nblk):
    return pl.BlockSpec((SCAN_BATCH, TB_SCAN, width), lambda g, ph, j: (g, _phase0_blk(ph, j, nblk), 0))


def _seq_phase1_spec(width, nblk):
    return pl.BlockSpec((SCAN_BATCH, TB_SCAN, width), lambda g, ph, j: (g, jnp.where(ph == 0, 0, j), 0))


def _state_spec(shape):
    return pl.BlockSpec((SCAN_BATCH,) + shape, lambda g, ph, j: (g,) + (0,) * len(shape))


def _param_spec(shape):
    nd = len(shape)
    return pl.BlockSpec(shape, lambda b, ph, j: (0,) * nd, pipeline_mode=pl.Buffered(1))


def _scan_consts():
    tri3 = jnp.stack([jnp.concatenate([_chunk_tri(TB_SCAN, lower=(d == 0))] * 3, axis=1) for d in range(2)])
    same_head = (_block_id((GLA_WIDTH, GLA_WIDTH), 0, GLA_DV)
                 == _block_id((GLA_WIDTH, GLA_WIDTH), 1, GLA_DV)).astype(BF16)
    gla_block = (_block_id((GLA_WIDTH, GLA_QKP), 0, GLA_DV)
                 == _block_id((GLA_WIDTH, GLA_QKP), 1, GLA_DKP)).astype(F32)
    expand = [jnp.where(_iota((LANE, SSD_WIDTH), 0) == d * SSD_HEADS + _block_id((LANE, SSD_WIDTH), 1, SSD_HEADDIM),
                        1.0, 0.0).astype(BF16) for d in range(2)]
    return {
        "tri3": tri3,
        "same_head2": jnp.concatenate([same_head] * 2, axis=0),
        "gla_block": gla_block,
        "expand3": jnp.concatenate([jnp.concatenate([e] * 3, axis=0) for e in expand], axis=1),
    }


def _scan_params():
    return pltpu.CompilerParams(dimension_semantics=("arbitrary", "arbitrary", "arbitrary"),
                                vmem_limit_bytes=VMEM_LIMIT)


def _gla_kernel(qk_ref, v_ref, r_ref, glr_ref, s0_ref, wg_ref, bg_ref, ng_ref, tri_ref, sameh_ref, bmask_ref,
                y_ref, sfin_ref, st_ref, ob_ref, qkb_ref, *, nblk):
    ph = pl.program_id(1)
    j = pl.program_id(2)
    nch = TB_SCAN // CHUNK
    scale = GLA_DK ** -0.5

    lane_k = _block_id((1, GLA_QKP), 1, GLA_DKP)
    lane_v = _block_id((1, GLA_WIDTH), 1, GLA_DV)
    kmask16 = [(lane_k == h).astype(BF16) for h in range(GLA_HEADS)]
    vmask16 = [(lane_v == h).astype(BF16) for h in range(GLA_HEADS)]
    blockmask = bmask_ref[...]

    def decay_cum(s, d):
        g = _dot(glr_ref[s], wg_ref[d]) + bg_ref[d]
        return _dot_exact_lhs(tri_ref[d], jax.nn.log_sigmoid(g) * (1.0 / GLA_TAU))

    def rows(x, c):
        return x[c * CHUNK:(c + 1) * CHUNK]

    def per_head_rows(x, masks):
        return jnp.concatenate([x * m for m in masks], axis=0)

    def backward_sweep(s):
        base = pl.multiple_of((nblk - 1 - j) * TB_SCAN, TB_SCAN)
        cum = decay_cum(s, 1)
        yield
        q_dec = (qk_ref[s, :, :GLA_QKP] * scale * jnp.exp(cum)).astype(BF16)
        k_inv = qk_ref[s, :, GLA_QKP:] * jnp.exp(-cum)
        v = v_ref[s]
        qkb_ref[s, pl.ds(base, TB_SCAN), :GLA_QKP] = q_dec
        qkb_ref[s, pl.ds(base, TB_SCAN), GLA_QKP:] = k_inv.astype(BF16)
        state = st_ref[s]
        yield
        for c in reversed(range(nch)):
            ob_ref[s, pl.ds(base + c * CHUNK, CHUNK), :] = _dot_nt(rows(q_dec, c), state)
            yield
            decay = jnp.exp(rows(cum, c)[0:1, :])
            contrib = _dot_tn(rows(v, c), rows(k_inv, c) * decay)
            yield
            state = state * decay + contrib * blockmask
            yield
        st_ref[s] = state

    def forward_sweep(s):
        base = pl.multiple_of(j * TB_SCAN, TB_SCAN)
        row_i = _iota((CHUNK, GLA_HEADS * CHUNK), 0)
        lane_j = _iota((CHUNK, GLA_HEADS * CHUNK), 1) & (CHUNK - 1)
        cum_f = decay_cum(s, 0)
        yield
        q_f = qk_ref[s, :, :GLA_QKP] * scale * jnp.exp(cum_f)
        yield
        k_f = qk_ref[s, :, GLA_QKP:] * jnp.exp(-cum_f)
        k_f16 = k_f.astype(BF16)
        yield
        v = v_ref[s]
        v16 = v.astype(BF16)
        q_b = qkb_ref[s, pl.ds(base, TB_SCAN), :GLA_QKP]
        k_b = qkb_ref[s, pl.ds(base, TB_SCAN), GLA_QKP:]
        state = st_ref[s]
        outs = []
        yield
        for c in range(nch):
            a_f = _dot_nt(rows(q_f, c), per_head_rows(rows(k_f16, c), kmask16))
            yield
            a_b = _dot_nt(rows(q_b, c), per_head_rows(rows(k_b, c), kmask16))
            yield
            scores = jnp.where(lane_j <= row_i, a_f, 0.0) + jnp.where(lane_j >= row_i, a_b, 0.0)
            o = _dot(scores, per_head_rows(rows(v16, c), vmask16))
            yield
            o = o + _dot_nt(rows(q_f, c), state)
            outs.append(o)
            yield
            decay = jnp.exp(rows(cum_f, c)[CHUNK - 1:CHUNK, :])
            contrib = _dot_tn(rows(v, c), rows(k_f, c) * decay)
            yield
            state = state * decay + contrib * blockmask
            yield
        st_ref[s] = state
        o = jnp.concatenate(outs, axis=0) + ob_ref[s, pl.ds(base, TB_SCAN), :]
        mean_sq = _dot_exact_rhs(o * o, sameh_ref[...], pieces=2) * (1.0 / GLA_DV)
        yield
        y_ref[s] = o * lax.rsqrt(mean_sq + EPS) * ng_ref[...] * _silu(r_ref[s])

    @pl.when(j == 0)
    def _():
        st_ref[...] = jnp.where(ph == 0, s0_ref[:, 1], s0_ref[:, 0])

    @pl.when(ph == 0)
    def _():
        _interleave(*(backward_sweep(s) for s in range(SCAN_BATCH)))

    @pl.when(ph == 1)
    def _():
        _interleave(*(forward_sweep(s) for s in range(SCAN_BATCH)))

    @pl.when(j == nblk - 1)
    def _():
        @pl.when(ph == 0)
        def _():
            sfin_ref[:, 1] = st_ref[...]

        @pl.when(ph == 1)
        def _():
            sfin_ref[:, 0] = st_ref[...]


def _gla(qk, v, r, glr, s0, wg, bg, ng, consts):
    batch, seq, _ = qk.shape
    nblk = seq // TB_SCAN
    sshape = (2, GLA_WIDTH, GLA_QKP)
    tri3, same_head2, gla_block = consts["tri3"], consts["same_head2"], consts["gla_block"]
    return pl.pallas_call(
        functools.partial(_gla_kernel, nblk=nblk),
        grid=(batch // SCAN_BATCH, 2, nblk),
        in_specs=[_seq_spec(P_QK, nblk), _seq_spec(GLA_WIDTH, nblk), _seq_phase1_spec(GLA_WIDTH, nblk),
                  _seq_spec(P_GLR, nblk), _state_spec(sshape),
                  _param_spec(wg.shape), _param_spec(bg.shape), _param_spec(ng.shape),
                  _param_spec(tri3.shape), _param_spec(same_head2.shape), _param_spec(gla_block.shape)],
        out_specs=[_seq_phase1_spec(GLA_WIDTH, nblk), _state_spec(sshape)],
        out_shape=[jax.ShapeDtypeStruct((batch, seq, GLA_WIDTH), F32),
                   jax.ShapeDtypeStruct((batch,) + sshape, F32)],
        scratch_shapes=[pltpu.VMEM((SCAN_BATCH, GLA_WIDTH, GLA_QKP), F32),
                        pltpu.VMEM((SCAN_BATCH, seq, GLA_WIDTH), F32),
                        pltpu.VMEM((SCAN_BATCH, seq, P_QK), BF16)],
        compiler_params=_scan_params(),
        name="gla",
    )(qk, v, r, glr, s0, wg, bg, ng, tri3, same_head2, gla_block)


def _ssd_kernel(z_ref, x_ref, xp_ref, xn_ref, dt_ref, s0_ref, cw_ref, cb_ref, dtb_ref, alog_ref, dsk_ref, ng_ref,
                tri_ref, exp_ref, y_ref, sfin_ref, st_ref, yb_ref, ext_ref, xs_ref, bc_ref, *, nblk):
    ph = pl.program_id(1)
    j = pl.program_id(2)
    blk = _scan_blk(ph, j, nblk)
    base = pl.multiple_of(blk * TB_SCAN, TB_SCAN)
    nch = TB_SCAN // CHUNK

    lane_h = _block_id((1, SSD_WIDTH), 1, SSD_HEADDIM)
    lane_g = _block_id((1, SSD_WIDTH), 1, SSD_HEADDIM * SSD_REP)
    lane_n = _block_id((1, SSD_BC), 1, SSD_STATE)
    head_lanes = [(lane_h == h).astype(BF16) for h in range(SSD_HEADS)]
    group_lanes = [(lane_g == g).astype(BF16) for g in range(SSD_GROUPS)]
    group_states = [(lane_n == g).astype(BF16) for g in range(SSD_GROUPS)]

    def rows(x, c):
        return x[c * CHUNK:(c + 1) * CHUNK]

    def conv_silu(s):
        ext_ref[s, 0:HALO, :] = jnp.where(blk > 0, xp_ref[s], 0.0)
        ext_ref[s, HALO:HALO + TB_SCAN, :] = x_ref[s]
        ext_ref[s, HALO + TB_SCAN:, :] = jnp.where(blk < nblk - 1, xn_ref[s], 0.0)
        pad = (SSD_CONV - 1) // 2
        acc = cb_ref[...]
        for t in range(SSD_CONV):
            acc = acc + ext_ref[s, pl.ds(HALO - pad + t, TB_SCAN), :] * cw_ref[t:t + 1, :]
        return _silu(acc)

    def decay_inputs(s):
        dt_c = jax.nn.softplus(dt_ref[s] + dtb_ref[...])
        return dt_c, dt_c * -jnp.exp(alog_ref[...])

    def expanded(d, cum_c, dt_c):
        both = _dot_exact_rhs(jnp.concatenate([cum_c, dt_c], axis=0), exp_ref[:, d * SSD_WIDTH:(d + 1) * SSD_WIDTH])
        return both[:TB_SCAN], both[TB_SCAN:]

    def expanded_both(cum_cf, cum_cb, dt_c):
        cum_c = jnp.where(_iota((TB_SCAN, LANE), 1) < SSD_HEADS, cum_cf, cum_cb)
        both = _dot_exact_rhs(jnp.concatenate([cum_c, dt_c], axis=0), exp_ref[...])
        return (both[:TB_SCAN, :SSD_WIDTH], both[TB_SCAN:, :SSD_WIDTH],
                both[:TB_SCAN, SSD_WIDTH:], both[TB_SCAN:, SSD_WIDTH:])

    def scan_terms(d, cum_e, dt_e):
        cum = cum_e.reshape(nch, CHUNK, SSD_WIDTH)
        last = cum[:, CHUNK - 1:CHUNK, :] if d == 0 else cum[:, 0:1, :]
        weight = jnp.exp(last - cum).reshape(TB_SCAN, SSD_WIDTH) * dt_e
        return jnp.exp(cum_e), weight, jnp.exp(last)

    def advance(state, decay, bm_c, xw_c):
        xw16 = xw_c.astype(BF16)
        contrib = [_dot_tn(bm_c[:, g * SSD_STATE:(g + 1) * SSD_STATE], xw16 * group_lanes[g])
                   for g in range(SSD_GROUPS)]
        return state * decay + jnp.concatenate(contrib, axis=0)

    def backward_sweep(s):
        xbc = conv_silu(s)
        yield
        dt_c, la_c = decay_inputs(s)
        cum_c = _dot_exact_lhs(tri_ref[1], la_c)
        yield
        xs = xbc[:, :SSD_WIDTH]
        bc16 = xbc[:, SSD_WIDTH:].astype(BF16)
        xs_ref[s, pl.ds(base, TB_SCAN), :] = xs
        bc_ref[s, pl.ds(base, TB_SCAN), :] = bc16
        bm, cm = bc16[:, :SSD_BC], bc16[:, SSD_BC:]
        cum_e, dt_e = expanded(1, cum_c, dt_c)
        yield
        grow, weight, decay = scan_terms(1, cum_e, dt_e)
        xw = xs * weight
        state = st_ref[s]
        outs = [None] * nch
        yield
        for c in reversed(range(nch)):
            outs[c] = _dot(rows(cm, c), state)
            state = advance(state, decay[c], rows(bm, c), rows(xw, c))
            yield
        st_ref[s] = state
        yb_ref[s, pl.ds(base, TB_SCAN), :] = jnp.concatenate(outs, axis=0) * grow

    def forward_sweep(s):
        xs = xs_ref[s, pl.ds(base, TB_SCAN), :]
        bm = bc_ref[s, pl.ds(base, TB_SCAN), :SSD_BC]
        cm = bc_ref[s, pl.ds(base, TB_SCAN), SSD_BC:]
        dt_c, la_c = decay_inputs(s)
        pieces = jnp.concatenate(_split3(la_c), axis=0)
        cum_cf = jnp.dot(tri_ref[0], pieces, preferred_element_type=F32)
        cum_cb = jnp.dot(tri_ref[1], pieces, preferred_element_type=F32)
        yield
        cum_f, dt_f, cum_b, dt_b = expanded_both(cum_cf, cum_cb, dt_c)
        yield
        grow, weight, decay = scan_terms(0, cum_f, dt_f)
        xw = xs * weight
        ci = _iota((CHUNK, SSD_WIDTH), 0)
        cj = _iota((CHUNK, SSD_WIDTH), 1) & (CHUNK - 1)
        diag = (_iota((TB_SCAN, SSD_WIDTH), 1) & (CHUNK - 1)) == (_iota((TB_SCAN, SSD_WIDTH), 0) & (CHUNK - 1))

        def at_j(x):
            return jnp.sum(jnp.where(diag, x, 0.0).reshape(nch, CHUNK, SSD_WIDTH), axis=1, keepdims=True)

        cum_row_f, cum_row_b, dt_row_f, dt_row_b = at_j(cum_f), at_j(cum_b), at_j(dt_f), at_j(dt_b)
        xs16 = xs.astype(BF16)
        state = st_ref[s]
        outs = []
        yield
        for c in range(nch):
            bm_group = [rows(bm, c) * group_states[g] for g in range(SSD_GROUPS)]
            bm_heads = jnp.concatenate([bm_group[h // SSD_REP] for h in range(SSD_HEADS)], axis=0)
            scores = _dot_nt(rows(cm, c), bm_heads)
            yield
            l_f = jnp.exp(jnp.where(cj <= ci, rows(cum_f, c) - cum_row_f[c], -jnp.inf))
            l_b = jnp.exp(jnp.where(cj >= ci, rows(cum_b, c) - cum_row_b[c], -jnp.inf))
            mat = scores * (l_f * dt_row_f[c] + l_b * dt_row_b[c])
            x_heads = jnp.concatenate([rows(xs16, c) * head_lanes[h] for h in range(SSD_HEADS)], axis=0)
            outs.append(_dot(mat, x_heads) + _dot(rows(cm, c), state) * rows(grow, c))
            yield
            state = advance(state, decay[c], rows(bm, c), rows(xw, c))
            yield
        st_ref[s] = state
        y = jnp.concatenate(outs, axis=0) + yb_ref[s, pl.ds(base, TB_SCAN), :] + xs * dsk_ref[...]
        y_ref[s] = _rms(y * _silu(z_ref[s]), ng_ref[...])

    @pl.when(j == 0)
    def _():
        st_ref[...] = jnp.where(ph == 0, s0_ref[:, 1], s0_ref[:, 0])

    @pl.when(ph == 0)
    def _():
        _interleave(*(backward_sweep(s) for s in range(SCAN_BATCH)))

    @pl.when(ph == 1)
    def _():
        _interleave(*(forward_sweep(s) for s in range(SCAN_BATCH)))

    @pl.when(j == nblk - 1)
    def _():
        @pl.when(ph == 0)
        def _():
            sfin_ref[:, 1] = st_ref[...]

        @pl.when(ph == 1)
        def _():
            sfin_ref[:, 0] = st_ref[...]


def _ssd(z, xbc, dt, s0, cw, cb, dtb, alog, dsk, ng, consts):
    batch, seq, _ = z.shape
    nblk = seq // TB_SCAN
    tiles = TB_SCAN // HALO
    sshape = (2, SSD_BC, SSD_WIDTH)

    def prev_map(g, ph, j):
        return (g, jnp.maximum(_phase0_blk(ph, j, nblk) * tiles - 1, 0), 0)

    def next_map(g, ph, j):
        return (g, jnp.minimum((_phase0_blk(ph, j, nblk) + 1) * tiles, seq // HALO - 1), 0)

    halo = (SCAN_BATCH, HALO, SSD_XBC)
    tri3, expand3 = consts["tri3"], consts["expand3"]
    return pl.pallas_call(
        functools.partial(_ssd_kernel, nblk=nblk),
        grid=(batch // SCAN_BATCH, 2, nblk),
        in_specs=[_seq_phase1_spec(SSD_WIDTH, nblk), _seq_phase0_spec(SSD_XBC, nblk),
                  pl.BlockSpec(halo, prev_map), pl.BlockSpec(halo, next_map),
                  _seq_spec(P_DT, nblk), _state_spec(sshape),
                  _param_spec(cw.shape), _param_spec(cb.shape), _param_spec(dtb.shape), _param_spec(alog.shape),
                  _param_spec(dsk.shape), _param_spec(ng.shape), _param_spec(tri3.shape), _param_spec(expand3.shape)],
        out_specs=[_seq_phase1_spec(SSD_WIDTH, nblk), _state_spec(sshape)],
        out_shape=[jax.ShapeDtypeStruct((batch, seq, SSD_WIDTH), F32),
                   jax.ShapeDtypeStruct((batch,) + sshape, F32)],
        scratch_shapes=[pltpu.VMEM((SCAN_BATCH, SSD_BC, SSD_WIDTH), F32),
                        pltpu.VMEM((SCAN_BATCH, seq, SSD_WIDTH), F32),
                        pltpu.VMEM((SCAN_BATCH, TB_SCAN + 2 * HALO, SSD_XBC), F32),
                        pltpu.VMEM((SCAN_BATCH, seq, SSD_WIDTH), F32),
                        pltpu.VMEM((SCAN_BATCH, seq, 2 * SSD_BC), BF16)],
        compiler_params=_scan_params(),
        name="ssd",
    )(z, xbc, xbc, xbc, dt, s0, cw, cb, dtb, alog, dsk, ng, tri3, expand3)


def _s5_kernel(*refs, reverse, final, batch):
    if final:
        (u_ref, x0_ref, bblk_ref, a_ref, cblk_ref, yprev_ref, d_ref, wglu_ref, bglu_ref,
         y_ref, xfin_ref, xs_ref, st_ref) = refs
    else:
        u_ref, x0_ref, bblk_ref, a_ref, cblk_ref, y_ref, xfin_ref, xs_ref, st_ref = refs
    i = pl.program_id(0)
    rows = TT_S5 * batch

    @pl.when(i == 0)
    def _():
        st_ref[...] = x0_ref[...]

    u = u_ref[...].reshape(rows, S5_WIDTH)
    u16 = u.astype(BF16)
    piece = 2 * S5_HALF // S5_PARTS
    readout = [None] * S5_STREAMS

    def stream(h):
        lo = h * 2 * S5_HALF
        re = slice(lo, lo + S5_HALF)
        im = slice(lo + S5_HALF, lo + 2 * S5_HALF)
        for p in range(S5_PARTS):
            cols = slice(lo + p * piece, lo + (p + 1) * piece)
            xs_ref[:, cols] = jnp.dot(u16, bblk_ref[:, cols], preferred_element_type=F32)
            yield
        a_re = jnp.broadcast_to(a_ref[0:1, h * S5_HALF:(h + 1) * S5_HALF], (batch, S5_HALF))
        a_im = jnp.broadcast_to(a_ref[1:2, h * S5_HALF:(h + 1) * S5_HALF], (batch, S5_HALF))
        x_re = st_ref[:, re]
        x_im = st_ref[:, im]
        for n in range(TT_S5):
            t = (TT_S5 - 1 - n) if reverse else n
            row = slice(t * batch, (t + 1) * batch)
            x_re, x_im = (a_re * x_re - a_im * x_im + xs_ref[row, re],
                          a_re * x_im + a_im * x_re + xs_ref[row, im])
            xs_ref[row, re] = x_re
            xs_ref[row, im] = x_im
            if (n + 1) % (TT_S5 // S5_PARTS) == 0:
                yield
        st_ref[:, re] = x_re
        st_ref[:, im] = x_im
        xfin_ref[:, re] = x_re
        xfin_ref[:, im] = x_im
        acc = None
        for p in range(S5_PARTS):
            cols = slice(lo + p * piece, lo + (p + 1) * piece)
            part = jnp.dot(xs_ref[:, cols].astype(BF16), cblk_ref[cols, :], preferred_element_type=F32)
            acc = part if acc is None else acc + part
            yield
        readout[h] = acc

    _interleave(*(_lagged(stream(h), h * S5_PARTS) for h in range(S5_STREAMS)))
    y = sum(readout[1:], readout[0])
    if final:
        y = y + yprev_ref[...].reshape(rows, S5_WIDTH) + u * d_ref[...]
        g = jax.nn.gelu(y)
        y = g * jax.nn.sigmoid(_dot(g, wglu_ref[...]) + bglu_ref[...])
    y_ref[...] = y.reshape(TT_S5, batch, S5_WIDTH)


def _s5_dir(u, x0, bblk, a, cblk, extra, reverse):
    seq, batch, _ = u.shape
    nblk = seq // TT_S5
    final = extra is not None

    def seq_map(i):
        return ((nblk - 1 - i) if reverse else i, 0, 0)

    def const(shape):
        nd = len(shape)
        return pl.BlockSpec(shape, lambda i: (0,) * nd)

    seq_spec = pl.BlockSpec((TT_S5, batch, S5_WIDTH), seq_map)
    in_specs = [seq_spec, const(x0.shape), const(bblk.shape), const(a.shape), const(cblk.shape)]
    args = [u, x0, bblk, a, cblk]
    if final:
        yprev, d, wglu, bglu = extra
        in_specs += [seq_spec, const(d.shape), const(wglu.shape), const(bglu.shape)]
        args += [yprev, d, wglu, bglu]
    return pl.pallas_call(
        functools.partial(_s5_kernel, reverse=reverse, final=final, batch=batch),
        grid=(nblk,),
        in_specs=in_specs,
        out_specs=[seq_spec, const(x0.shape)],
        out_shape=[jax.ShapeDtypeStruct(u.shape, F32), jax.ShapeDtypeStruct(x0.shape, F32)],
        scratch_shapes=[pltpu.VMEM((TT_S5 * batch, 2 * S5_LANES), F32), pltpu.VMEM(x0.shape, F32)],
        compiler_params=pltpu.CompilerParams(dimension_semantics=("arbitrary",), vmem_limit_bytes=VMEM_LIMIT),
        name="s5_bwd" if reverse else "s5_fwd",
    )(*args)


def _s5(u_tm, x0, prm):
    y_b, xf_b = _s5_dir(u_tm, x0[1], prm["bblk"][1], prm["a"][1], prm["cblk"], None, True)
    y, xf_f = _s5_dir(u_tm, x0[0], prm["bblk"][0], prm["a"][0], prm["cblk"],
                      (y_b, prm["d"], prm["wglu"], prm["bglu"]), False)
    return y, jnp.stack([xf_f, xf_b])


def _s5_zoh_kernel(are_ref, aim_ref, ldt_ref, bre_ref, bim_ref, abre_ref, abim_ref, bbre_ref, bbim_ref):
    a_re = are_ref[...]
    a_im = aim_ref[...]
    dt = jnp.exp(ldt_ref[...])
    mag = jnp.exp(dt * a_re)
    ab_re = mag * jnp.cos(dt * a_im)
    ab_im = mag * jnp.sin(dt * a_im)
    den = a_re * a_re + a_im * a_im
    num_re = ab_re - 1.0
    num_im = ab_im
    f_re = (num_re * a_re + num_im * a_im) / den
    f_im = (num_im * a_re - num_re * a_im) / den
    abre_ref[...] = ab_re
    abim_ref[...] = ab_im
    b_re = bre_ref[...]
    b_im = bim_ref[...]
    for d in range(2):
        bbre_ref[d] = f_re[d:d + 1] * b_re - f_im[d:d + 1] * b_im
        bbim_ref[d] = f_re[d:d + 1] * b_im + f_im[d:d + 1] * b_re


def _s5_params(a_re, a_im, log_dt, b_re, b_im, c_re, c_im, d, w_glu, b_glu):
    lanes = lambda t: t.reshape(2, S5_LANES)
    b_lanes = lambda b: b.transpose(2, 0, 1).reshape(S5_GROUP, S5_LANES)
    vec = jax.ShapeDtypeStruct((2, S5_LANES), F32)
    mat = jax.ShapeDtypeStruct((2, S5_GROUP, S5_LANES), F32)
    ab_re, ab_im, bb_re, bb_im = pl.pallas_call(
        _s5_zoh_kernel, out_shape=[vec, vec, mat, mat], name="s5_zoh",
    )(lanes(a_re), lanes(a_im), lanes(jnp.repeat(log_dt, S5_STATE, axis=-1)), b_lanes(b_re), b_lanes(b_im))
    eye = jnp.eye(S5_GROUPS, dtype=F32)

    def in_block(bb):
        bb = bb.reshape(2, S5_GROUP, S5_GROUPS, S5_STATE)
        return jnp.einsum("dhgp,gk->dghkp", bb, eye).reshape(2, S5_WIDTH, S5_LANES)

    def out_block(c):
        return jnp.einsum("ghp,gk->gpkh", c, eye).reshape(S5_LANES, S5_WIDTH)

    def stream_major(re, im, axis):
        parts = []
        for h in range(S5_STREAMS):
            idx = [slice(None)] * re.ndim
            idx[axis] = slice(h * S5_HALF, (h + 1) * S5_HALF)
            parts += [re[tuple(idx)], im[tuple(idx)]]
        return jnp.concatenate(parts, axis=axis)

    return {
        "a": jnp.stack([ab_re.reshape(2, S5_LANES), ab_im.reshape(2, S5_LANES)], axis=1),
        "bblk": stream_major(in_block(bb_re), in_block(bb_im), 2).astype(BF16),
        "cblk": stream_major(out_block(c_re), -out_block(c_im), 0).astype(BF16),
        "d": d.reshape(1, S5_WIDTH),
        "wglu": w_glu.astype(BF16),
        "bglu": b_glu.reshape(1, S5_WIDTH),
    }


def _pad_heads(w, heads, width, padded):
    lead = w.shape[:-1]
    w = w.reshape(lead + (heads, width))
    w = jnp.pad(w, [(0, 0)] * len(lead) + [(0, 0), (0, padded - width)])
    return w.reshape(lead + (heads * padded,))


def _pad_last(w, width):
    return jnp.pad(w, [(0, 0)] * (w.ndim - 1) + [(0, width - w.shape[-1])])


def _pack_w_in(w_in):
    pts = []
    acc = 0
    for s in IN_SPLITS:
        pts.append((acc, acc + s))
        acc += s
    q, k, v, r, glr, s5, z, xbc, dt = (w_in[..., a:b] for a, b in pts)
    return jnp.concatenate([
        _pad_heads(q, GLA_HEADS, GLA_DK, GLA_DKP), _pad_heads(k, GLA_HEADS, GLA_DK, GLA_DKP), v, r,
        _pad_last(glr, P_GLR), s5, z, xbc, _pad_last(dt, P_DT)], axis=-1).astype(BF16)


def _pack_gla_gate(w_gate, b_gate):
    w = _pad_heads(w_gate, GLA_HEADS, GLA_DK, GLA_DKP)
    wp = jnp.zeros((2, P_GLR, GLA_QKP), F32)
    for d in range(2):
        wp = wp.at[d, d * GLA_RANK:(d + 1) * GLA_RANK].set(w[d])
    return wp.astype(BF16), _pad_heads(b_gate, GLA_HEADS, GLA_DK, GLA_DKP).reshape(2, 1, GLA_QKP)


def _per_seq(t, batch):
    n, ch = t.shape
    return t.reshape(batch, n // batch, ch)


def _to_col_major(t):
    batch, n, ch = t.shape
    return t.reshape(batch, n // GRID_W, GRID_W, ch).transpose(0, 2, 1, 3).reshape(batch, n, ch)


def _from_col_major(t):
    batch, n, ch = t.shape
    return t.reshape(batch, GRID_W, n // GRID_W, ch).transpose(0, 2, 1, 3).reshape(batch, n, ch)


def _to_time_major(t, batch):
    n, ch = t.shape
    return t.reshape(batch, n // batch, ch).transpose(1, 0, 2)


def _from_time_major(t):
    seq, batch, ch = t.shape
    return t.transpose(1, 0, 2).reshape(seq * batch, ch)


def kernel(x, c, ctx, c_ctx, ada_w, ada_b, norm_g, w_in, w_out, ff_w_gate, ff_w_up, ff_w_down, gla_w_gate, gla_b_gate, gla_norm_g, s5_a_re, s5_a_im, s5_log_dt, s5_b_re, s5_b_im, s5_c_re, s5_c_im, s5_d, s5_w_glu, s5_b_glu, ssd_conv_w, ssd_conv_b, ssd_dt_bias, ssd_a_log, ssd_d, ssd_norm_g, final_norm_g):
    batch, seq, _ = x.shape
    ctx_len = ctx.shape[1]
    depth = ada_w.shape[0]
    assert seq == GRID_W * CHUNK and ctx_len % TB_SCAN == 0 and seq % TM_TOKENS == 0
    assert batch == SUBLANE and (batch * ctx_len) % TM_TOKENS == 0 and batch % SCAN_BATCH == 0

    mod_rows = 2 * SUBLANE
    cc = jnp.zeros((mod_rows, D_MODEL), F32).at[:batch].set(c).at[batch].set(c_ctx)
    mods = _ada(cc, ada_w, ada_b).reshape(depth, mod_rows, N_MOD, D_MODEL)
    lat_mod = _mod_spec(seq // TM_TOKENS, None)
    ctx_mod = _mod_spec(None, batch)

    h = x.reshape(batch * seq, D_MODEL)
    hc = ctx.reshape(batch * ctx_len, D_MODEL)
    fg = final_norm_g.reshape(1, D_MODEL)
    wg, wu, wd = ff_w_gate.astype(BF16), ff_w_up.astype(BF16), ff_w_down.astype(BF16)
    w_all = _pack_w_in(w_in)
    wo = w_out.astype(BF16)
    consts = _scan_consts()

    for i in range(depth):
        ctx_out = i < depth - 1
        mod = mods[i]
        ng = norm_g[i]
        gla_wg, gla_bg = _pack_gla_gate(gla_w_gate[i], gla_b_gate[i])
        gla_ng = gla_norm_g[i].reshape(1, GLA_WIDTH)
        s5p = _s5_params(s5_a_re[i], s5_a_im[i], s5_log_dt[i], s5_b_re[i], s5_b_im[i], s5_c_re[i], s5_c_im[i],
                         s5_d[i], s5_w_glu[i], s5_b_glu[i])
        cw = jnp.pad(ssd_conv_w[i], ((0, SUBLANE - SSD_CONV), (0, 0)))
        cb = ssd_conv_b[i].reshape(1, SSD_XBC)
        dtb = _pad_last(ssd_dt_bias[i].reshape(1, 2 * SSD_HEADS), P_DT)
        alog = _pad_last(ssd_a_log[i].reshape(1, 2 * SSD_HEADS), P_DT)
        dsk = jnp.repeat(ssd_d[i], SSD_HEADDIM).reshape(1, SSD_WIDTH)
        ssd_ng = ssd_norm_g[i].reshape(1, SSD_WIDTH)

        h = _ffn1(h, mod, lat_mod, ng, wg, wu, wd, i)
        hc = _ffn1(hc, mod, ctx_mod, ng, wg, wu, wd, i)

        qk_l, v_l, r_l, glr_l, s5_l, z_l, xbc_l, dt_l = _inproj(h, mod, lat_mod, ng, w_all, i)
        qk_c, v_c, r_c, glr_c, s5_c, z_c, xbc_c, dt_c = _inproj(hc, mod, ctx_mod, ng, w_all, i)

        gla_s0 = jnp.zeros((batch, 2, GLA_WIDTH, GLA_QKP), F32)
        seqs = functools.partial(_per_seq, batch=batch)
        yg_c, gla_s = _gla(seqs(qk_c), seqs(v_c), seqs(r_c), seqs(glr_c), gla_s0, gla_wg, gla_bg, gla_ng, consts)
        yg_l, _ = _gla(seqs(qk_l), seqs(v_l), seqs(r_l), seqs(glr_l), gla_s, gla_wg, gla_bg, gla_ng, consts)
        yg_c = yg_c.reshape(-1, GLA_WIDTH)
        yg_l = yg_l.reshape(-1, GLA_WIDTH)

        s5_x0 = jnp.zeros((2, batch, 2 * S5_LANES), F32)
        ys_c, s5_x = _s5(_to_time_major(s5_c, batch), s5_x0, s5p)
        ys_l, _ = _s5(_to_time_major(s5_l, batch), s5_x, s5p)
        ys_l = _from_time_major(ys_l)

        ssd_s0 = jnp.zeros((batch, 2, SSD_BC, SSD_WIDTH), F32)
        yc_c, ssd_s = _ssd(seqs(z_c), seqs(xbc_c), seqs(dt_c), ssd_s0, cw, cb, dtb, alog, dsk, ssd_ng, consts)
        yc_l, _ = _ssd(_to_col_major(seqs(z_l)), _to_col_major(seqs(xbc_l)), _to_col_major(seqs(dt_l)),
                       ssd_s, cw, cb, dtb, alog, dsk, ssd_ng, consts)
        yc_c = yc_c.reshape(-1, SSD_WIDTH)
        yc_l = _from_col_major(yc_l).reshape(-1, SSD_WIDTH)

        h = _mix_ffn2(h, yg_l, ys_l, yc_l, mod, lat_mod, ng, wo, wg, wu, wd, fg, i, final=not ctx_out)
        if ctx_out:
            hc = _mix_ffn2(hc, yg_c, _from_time_major(ys_c), yc_c, mod, ctx_mod, ng, wo, wg, wu, wd, fg, i,
                           final=False)
    return h.reshape(batch, seq, D_MODEL)
```
